```python
import jax
import jax.numpy as jnp
from jax import lax
import numpy as np

D_MODEL = 1024
BATCH = 16
SEQ = 2048
DEPTH = 1
DEC_BATCH = 32
DEC_SEQ = 16
PAST_LEN = 2048

CHUNK = 64
GLA_HEADS = 4
GLA_DK = D_MODEL // 2
GLA_DV = D_MODEL
GLA_DK_HEAD = GLA_DK // GLA_HEADS
GLA_DV_HEAD = GLA_DV // GLA_HEADS
GLA_GATE_RANK = 16
GLA_TAU = 16.0
GLA_BLOCK = CHUNK // 4
SB_HEADS = 16
SB_HD = 64
SB_DIM = SB_HEADS * SB_HD
SB_BLOCK = 128
PEER_HEADS = 8
PEER_DK = 256
PEER_HALF = PEER_DK // 2
N_KEYS = 128
N_EXPERTS = N_KEYS * N_KEYS
PEER_TOPK = 16
PEER_BLOCK = 128
N_MOD = 6
DN_ALPHA = (2.0 * DEPTH) ** 0.25
DN_BETA = (8.0 * DEPTH) ** -0.25
LN_EPS = 1e-5
RMS_EPS = 1e-6
SPLIT_SIZES = (GLA_DK, GLA_DK, GLA_DV, GLA_GATE_RANK, GLA_DV, SB_DIM, SB_DIM, SB_DIM, D_MODEL, D_MODEL)
IN_COLS = sum(SPLIT_SIZES)

kernel_name = 'stream_gla_stickbreak_peer'


def layer_norm(x, w=None, b=None):
    xf = x.astype(jnp.float32)
    mu = jnp.mean(xf, axis=-1, keepdims=True)
    var = jnp.mean(jnp.square(xf - mu), axis=-1, keepdims=True)
    y = (xf - mu) * lax.rsqrt(var + LN_EPS)
    if w is not None:
        y = y * w + b
    return y.astype(x.dtype)


def modulate(x, shift, scale):
    return layer_norm(x) * (1.0 + scale[:, None, :]) + shift[:, None, :]


def split_cols(p):
    out = []
    start = 0
    for size in SPLIT_SIZES:
        out.append(p[..., start:start + size])
        start += size
    return out


def gla_recurrence(q, k, v, log_a, s0):
    B, T, H = q.shape[0], q.shape[1], q.shape[2]
    pad = (-T) % GLA_BLOCK

    def blocks(t):
        t = jnp.pad(t.astype(jnp.float32), ((0, 0), (0, pad), (0, 0), (0, 0)))
        return jnp.moveaxis(t.reshape(B, -1, GLA_BLOCK, t.shape[2], t.shape[3]), 1, 0)

    causal = jnp.tril(jnp.ones((GLA_BLOCK, GLA_BLOCK), dtype=bool))

    def step(s, blk):
        qb, kb, vb, lab = blk
        b = jnp.cumsum(lab, axis=1)
        b_last = b[:, -1]
        qe = qb * jnp.exp(b)
        ke = kb * jnp.exp(-b)
        kd = kb * jnp.exp(b_last[:, None] - b)
        att = jnp.where(causal[None, None], jnp.einsum('blhk,bmhk->bhlm', qe, ke), 0.0)
        o = jnp.einsum('blhk,bhkv->blhv', qe, s) + jnp.einsum('bhlm,bmhv->blhv', att, vb)
        s_new = jnp.exp(b_last)[..., None] * s + jnp.einsum('blhk,blhv->bhkv', kd, vb)
        return s_new, o

    s_fin, o = lax.scan(step, s0.astype(jnp.float32), (blocks(q), blocks(k), blocks(v), blocks(log_a)))
    o = jnp.moveaxis(o, 0, 1).reshape(B, T + pad, H, v.shape[3])[:, :T]
    return o, s_fin


def stick_breaking(q, k, v, q_pos, k_pos):
    z = jnp.einsum('bqhd,bkhd->bhqk', q, k).astype(jnp.float32) * (SB_HD ** -0.5)
    mask = k_pos[None, :] < q_pos[:, None]
    log_rest = jnp.where(mask, jax.nn.log_sigmoid(-z), 0.0)
    after = lax.cumsum(log_rest, axis=3, reverse=True) - log_rest
    a = jnp.where(mask, jnp.exp(jax.nn.log_sigmoid(z) + after), 0.0)
    return jnp.einsum('bhqk,bkhd->bqhd', a.astype(v.dtype), v)


def peer_block(h, peer_wq, peer_keys, peer_u, peer_v):
    n = h.shape[0]
    q = (h @ peer_wq).reshape(n, PEER_HEADS, 2, PEER_HALF)
    s = jnp.einsum('nhpd,hpkd->nhpk', q, peer_keys).astype(jnp.float32)
    sc, ix = lax.top_k(s, PEER_TOPK)
    cand = (sc[:, :, 0, :, None] + sc[:, :, 1, None, :]).reshape(n, PEER_HEADS, PEER_TOPK * PEER_TOPK)
    cid = (ix[:, :, 0, :, None] * N_KEYS + ix[:, :, 1, None, :]).reshape(n, PEER_HEADS, PEER_TOPK * PEER_TOPK)
    top, pos = lax.top_k(cand, PEER_TOPK)
    eid = jnp.take_along_axis(cid, pos, axis=-1)
    g = jax.nn.softmax(top, axis=-1)
    act = jax.nn.gelu(jnp.einsum('nhkd,nd->nhk', peer_u[eid], h).astype(jnp.float32))
    w = (g * act).astype(h.dtype)
    return jnp.einsum('nhk,nhkd->nd', w, peer_v[eid])


def peer(h, peer_wq, peer_keys, peer_u, peer_v):
    B, T, D = h.shape
    n = B * T
    pad = (-n) % PEER_BLOCK
    hb = jnp.pad(h.reshape(n, D), ((0, pad), (0, 0))).reshape(-1, PEER_BLOCK, D)
    out = lax.map(lambda blk: peer_block(blk, peer_wq, peer_keys, peer_u, peer_v), hb)
    return out.reshape(-1, D)[:n].reshape(B, T, D)


def trunk_layer(x, c, gla_s0, k_past, v_past, w_ada, b_ada, w_in, w_gla_gate, b_gla_gate, gla_norm_w,
                w_gla_proj, w_sb_proj, w_o, ln1_w, ln1_b, peer_wq, peer_keys, peer_u, peer_v, ln2_w, ln2_b):
    B, T, _ = x.shape
    mod = jax.nn.silu(c) @ w_ada + b_ada
    sh1, sc1, g1, sh2, sc2, g2 = jnp.split(mod, N_MOD, axis=-1)

    h = modulate(x, sh1, sc1)
    gq, gk, gv, glr, gog, sq, sk, sv, mg_a, mg_b = split_cols(h @ w_in)

    q = gq.reshape(B, T, GLA_HEADS, GLA_DK_HEAD) * (GLA_DK_HEAD ** -0.5)
    k = gk.reshape(B, T, GLA_HEADS, GLA_DK_HEAD)
    v = gv.reshape(B, T, GLA_HEADS, GLA_DV_HEAD)
    log_a = (jax.nn.log_sigmoid((glr @ w_gla_gate + b_gla_gate).astype(jnp.float32)) / GLA_TAU
             ).reshape(B, T, GLA_HEADS, GLA_DK_HEAD)
    if gla_s0 is None:
        gla_s0 = jnp.zeros((B, GLA_HEADS, GLA_DK_HEAD, GLA_DV_HEAD), jnp.float32)
    o_gla, s_fin = gla_recurrence(q, k, v, log_a, gla_s0)
    o_gla = o_gla * lax.rsqrt(jnp.mean(jnp.square(o_gla), axis=-1, keepdims=True) + RMS_EPS) * gla_norm_w
    o_gla = o_gla.reshape(B, T, GLA_DV).astype(x.dtype) * jax.nn.silu(gog)
    branch_a = o_gla @ w_gla_proj

    sbq = sq.reshape(B, T, SB_HEADS, SB_HD)
    sbk = sk.reshape(B, T, SB_HEADS, SB_HD)
    sbv = sv.reshape(B, T, SB_HEADS, SB_HD)
    if k_past is None:
        outs = []
        for i in range(T // SB_BLOCK):
            lo, hi = i * SB_BLOCK, (i + 1) * SB_BLOCK
            outs.append(stick_breaking(sbq[:, lo:hi], sbk[:, :hi], sbv[:, :hi], jnp.arange(lo, hi), jnp.arange(hi)))
        o_sb = jnp.concatenate(outs, axis=1)
    else:
        past = k_past.shape[1]
        k_all = jnp.concatenate([k_past, sbk], axis=1)
        v_all = jnp.concatenate([v_past, sbv], axis=1)
        o_sb = stick_breaking(sbq, k_all, v_all, past + jnp.arange(T), jnp.arange(past + T))
    branch_b = o_sb.reshape(B, T, SB_DIM) @ w_sb_proj

    mix = (jax.nn.sigmoid(mg_a) * branch_a + jax.nn.sigmoid(mg_b) * branch_b) @ w_o
    x = layer_norm(DN_ALPHA * x + g1[:, None, :] * mix, ln1_w, ln1_b)

    h2 = modulate(x, sh2, sc2)
    f = peer(h2, peer_wq, peer_keys, peer_u, peer_v)
    x = layer_norm(DN_ALPHA * x + g2[:, None, :] * f, ln2_w, ln2_b)
    return x, s_fin.astype(x.dtype), sbk, sbv


def setup_inputs(seed: int = 0) -> dict:
    key = jax.random.key(seed)
    ks = jax.random.split(key, 26)

    def nrm(k, shape, scale):
        return jax.random.normal(k, shape, jnp.float32) * scale

    D = D_MODEL
    return {
        'x_prompt': nrm(ks[0], (BATCH, SEQ, D), 1.0),
        'x_sample': nrm(ks[1], (DEC_BATCH, DEC_SEQ, D), 1.0),
        'state_gla': nrm(ks[2], (DEPTH, DEC_BATCH, GLA_HEADS, GLA_DK_HEAD, GLA_DV_HEAD), 0.5),
        'cache_sb_k': nrm(ks[3], (DEPTH, DEC_BATCH, PAST_LEN, SB_HEADS, SB_HD), 1.0),
        'cache_sb_v': nrm(ks[4], (DEPTH, DEC_BATCH, PAST_LEN, SB_HEADS, SB_HD), 1.0),
        'c_prompt': nrm(ks[5], (BATCH, D), 1.0),
        'c_sample': nrm(ks[6], (DEC_BATCH, D), 1.0),
        'w_ada': nrm(ks[7], (DEPTH, D, N_MOD * D), D ** -0.5),
        'b_ada': nrm(ks[8], (DEPTH, N_MOD * D), 0.02),
        'w_in': nrm(ks[9], (DEPTH, D, IN_COLS), D ** -0.5),
        'w_gla_gate': nrm(ks[10], (DEPTH, GLA_GATE_RANK, GLA_DK), GLA_GATE_RANK ** -0.5),
        'b_gla_gate': nrm(ks[11], (DEPTH, GLA_DK), 0.1),
        'gla_norm_w': 1.0 + nrm(ks[12], (DEPTH, GLA_DV_HEAD), 0.02),
        'w_gla_proj': nrm(ks[13], (DEPTH, GLA_DV, D), DN_BETA * GLA_DV ** -0.5),
        'w_sb_proj': nrm(ks[14], (DEPTH, SB_DIM, D), DN_BETA * SB_DIM ** -0.5),
        'w_o': nrm(ks[15], (DEPTH, D, D), DN_BETA * D ** -0.5),
        'ln1_w': 1.0 + nrm(ks[16], (DEPTH, D), 0.02),
        'ln1_b': nrm(ks[17], (DEPTH, D), 0.02),
        'peer_wq': nrm(ks[18], (DEPTH, D, PEER_HEADS * PEER_DK), D ** -0.5),
        'peer_keys': nrm(ks[19], (DEPTH, PEER_HEADS, 2, N_KEYS, PEER_HALF), PEER_HALF ** -0.5),
        'peer_u': nrm(ks[20], (DEPTH, N_EXPERTS, D), D ** -0.5),
        'peer_v': nrm(ks[21], (DEPTH, N_EXPERTS, D), DN_BETA),
        'ln2_w': 1.0 + nrm(ks[22], (DEPTH, D), 0.02),
        'ln2_b': nrm(ks[23], (DEPTH, D), 0.02),
    }


def reference(x_prompt, x_sample, state_gla, cache_sb_k, cache_sb_v, c_prompt, c_sample,
              w_ada, b_ada, w_in, w_gla_gate, b_gla_gate, gla_norm_w, w_gla_proj, w_sb_proj, w_o,
              ln1_w, ln1_b, peer_wq, peer_keys, peer_u, peer_v, ln2_w, ln2_b):
    yp = x_prompt
    ys = x_sample
    sp_list, kp_list, vp_list, ss_list, ksl, vsl = [], [], [], [], [], []
    for l in range(DEPTH):
        params = (w_ada[l], b_ada[l], w_in[l], w_gla_gate[l], b_gla_gate[l], gla_norm_w[l], w_gla_proj[l],
                  w_sb_proj[l], w_o[l], ln1_w[l], ln1_b[l], peer_wq[l], peer_keys[l], peer_u[l], peer_v[l],
                  ln2_w[l], ln2_b[l])
        yp, sp, kp, vp = trunk_layer(yp, c_prompt, None, None, None, *params)
        ys, ss, kss, vss = trunk_layer(ys, c_sample, state_gla[l], cache_sb_k[l], cache_sb_v[l], *params)
        sp_list.append(sp)
        kp_list.append(kp)
        vp_list.append(vp)
        ss_list.append(ss)
        ksl.append(kss)
        vsl.append(vss)
    new_state_gla_prompt = jnp.stack(sp_list, axis=0)
    new_sb_k_prompt = jnp.stack(kp_list, axis=0)
    new_sb_v_prompt = jnp.stack(vp_list, axis=0)
    new_state_gla_sample = jnp.stack(ss_list, axis=0)
    new_sb_k_sample = jnp.stack(ksl, axis=0)
    new_sb_v_sample = jnp.stack(vsl, axis=0)
    return (yp, ys, new_state_gla_prompt, new_sb_k_prompt, new_sb_v_prompt,
            new_state_gla_sample, new_sb_k_sample, new_sb_v_sample)
```

```python
import functools

import jax
import jax.numpy as jnp
from jax import lax
from jax.experimental import pallas as pl
from jax.experimental.pallas import tpu as pltpu

F32 = jnp.float32
BF16 = jnp.bfloat16

D_MODEL = 1024
N_MOD = 6
GLA_HEADS = 4
GLA_DK_HEAD = 128
GLA_DV_HEAD = 256
GLA_DK = GLA_HEADS * GLA_DK_HEAD
GLA_GATE_RANK = 16
GLA_TAU = 16.0
GLA_BLOCK = 16
SB_HEADS = 16
SB_HD = 64
SB_PAIRS = SB_HEADS // 2
PEER_HEADS = 8
PEER_HALF = 128
N_KEYS = 128
N_EXPERTS = N_KEYS * N_KEYS
PEER_TOPK = 16
DEPTH = 1
DN_ALPHA = (2.0 * DEPTH) ** 0.25
LN_EPS = 1e-5
RMS_EPS = 1e-6

LANES = 128
VMEM_LIMIT = 48 * 1024 * 1024
NEG_INF = float("-inf")

_NT = (((1,), (1,)), ((), ()))


def _dot(a, b):
    return jnp.dot(a, b, preferred_element_type=F32)


def _dot_nt(a, b):
    return lax.dot_general(a, b, _NT, preferred_element_type=F32)


def _layer_norm(x):
    mu = jnp.mean(x, axis=-1, keepdims=True)
    xc = x - mu
    var = jnp.mean(xc * xc, axis=-1, keepdims=True)
    return xc * lax.rsqrt(var + LN_EPS)


def _log_sigmoid(x):
    return jnp.minimum(x, 0.0) - jnp.log1p(jnp.exp(-jnp.abs(x)))


def _split_hi_lo(x):
    hi = x.astype(BF16)
    lo = (x - hi.astype(F32)).astype(BF16)
    return hi, lo


def _params(semantics):
    return pltpu.CompilerParams(dimension_semantics=semantics, vmem_limit_bytes=VMEM_LIMIT)


def _mod_operand(vec, per_row, tile, tiles_per_seq, grid_rank):
    pad = (0,) * (grid_rank - 1)
    if per_row:
        return vec, pl.BlockSpec((tile, D_MODEL), lambda i, *_: (i, 0))
    arr = vec[:, None, :]
    return arr, pl.BlockSpec((None, 1, D_MODEL), lambda i, *_: (i // tiles_per_seq, 0, 0))


def _mod_kernel(c_ref, w_ref, b_ref, o_ref):
    c = c_ref[...]
    s = (c * jax.nn.sigmoid(c)).astype(BF16)
    o_ref[...] = _dot(s, w_ref[...].astype(BF16)) + b_ref[...]


def _modulation(c_all, w_ada, b_ada):
    rows = c_all.shape[0]
    return pl.pallas_call(
        _mod_kernel,
        grid=(N_MOD,),
        in_specs=[
            pl.BlockSpec((rows, D_MODEL), lambda j: (0, 0)),
            pl.BlockSpec((D_MODEL, D_MODEL), lambda j: (0, j)),
            pl.BlockSpec((1, D_MODEL), lambda j: (0, j)),
        ],
        out_specs=pl.BlockSpec((rows, D_MODEL), lambda j: (0, j)),
        out_shape=jax.ShapeDtypeStruct((rows, N_MOD * D_MODEL), F32),
        compiler_params=_params(("arbitrary",)),
        name="adaln_mod",
    )(c_all, w_ada, b_ada)


_N_COL_TILES = 8
_COL_SK = 4
_COL_SV = 5


def _inproj_kernel(x_ref, sh_ref, sc_ref, w_ref, wglr_ref, proj_ref, sbk_ref, sbv_ref, glr_ref, h_scr):
    j = pl.program_id(1)

    @pl.when(j == 0)
    def _():
        h = _layer_norm(x_ref[...]) * (1.0 + sc_ref[...]) + sh_ref[...]
        hb = h.astype(BF16)
        h_scr[...] = hb
        glr_ref[...] = _dot(hb, wglr_ref[...])

    acc = _dot(h_scr[...], w_ref[...])

    @pl.when(j == _COL_SK)
    def _():
        sbk_ref[...] = acc

    @pl.when(j == _COL_SV)
    def _():
        sbv_ref[...] = acc

    @pl.when((j < _COL_SK) | (j > _COL_SV))
    def _():
        proj_ref[...] = acc


def _proj_col(j):
    return jnp.where(j < _COL_SK, j, jnp.where(j <= _COL_SV, _COL_SK - 1, j - 2))


def _input_projection(x2d, shift, scale, per_row, seq_len, w_main, w_glr):
    n = x2d.shape[0]
    tm = min(1024, seq_len) if not per_row else min(512, n)
    tiles_per_seq = max(seq_len // tm, 1)
    sh_arr, sh_spec = _mod_operand(shift, per_row, tm, tiles_per_seq, 2)
    sc_arr, sc_spec = _mod_operand(scale, per_row, tm, tiles_per_seq, 2)
    row_out = lambda i, j: (i, 0)
    return pl.pallas_call(
        _inproj_kernel,
        grid=(n // tm, _N_COL_TILES),
        in_specs=[
            pl.BlockSpec((tm, D_MODEL), row_out),
            sh_spec,
            sc_spec,
            pl.BlockSpec((D_MODEL, D_MODEL), lambda i, j: (0, j)),
            pl.BlockSpec((D_MODEL, LANES), lambda i, j: (0, 0)),
        ],
        out_specs=[
            pl.BlockSpec((tm, D_MODEL), lambda i, j: (i, _proj_col(j))),
            pl.BlockSpec((tm, D_MODEL), row_out),
            pl.BlockSpec((tm, D_MODEL), row_out),
            pl.BlockSpec((tm, LANES), row_out),
        ],
        out_shape=[
            jax.ShapeDtypeStruct((n, 6 * D_MODEL), F32),
            jax.ShapeDtypeStruct((n, D_MODEL), F32),
            jax.ShapeDtypeStruct((n, D_MODEL), F32),
            jax.ShapeDtypeStruct((n, LANES), F32),
        ],
        scratch_shapes=[pltpu.VMEM((tm, D_MODEL), BF16)],
        compiler_params=_params(("arbitrary", "arbitrary")),
        name="ln_inproj",
    )(x2d, sh_arr, sc_arr, w_main, w_glr)


_GLA_CHUNK = 128


def _pad_rows(x, rows):
    if x.shape[0] == rows:
        return x
    return jnp.concatenate([x, jnp.zeros((rows - x.shape[0], x.shape[1]), x.dtype)], axis=0)


def _gla_kernel(*refs, t_real, has_s0):
    if has_s0:
        (qk_ref, v_ref, gog_ref, glr_ref, wg_ref, bg_ref, nw_ref, s0_ref, og_ref, sfin_ref, st_scr) = refs
    else:
        (qk_ref, v_ref, gog_ref, glr_ref, wg_ref, bg_ref, nw_ref, og_ref, sfin_ref, st_scr) = refs
        s0_ref = None
    t = pl.program_id(1)
    tc = _GLA_CHUNK
    n_blocks = t_real // GLA_BLOCK

    @pl.when(t == 0)
    def _():
        for hh in range(GLA_HEADS):
            if has_s0:
                st_scr[hh] = s0_ref[hh].T
            else:
                st_scr[hh] = jnp.zeros((GLA_DV_HEAD, GLA_DK_HEAD), F32)

    row = lax.broadcasted_iota(jnp.int32, (tc, tc), 0)
    col = lax.broadcasted_iota(jnp.int32, (tc, tc), 1)
    same_block = (row // GLA_BLOCK) == (col // GLA_BLOCK)
    causal = same_block & (col <= row)
    m_cum = causal.astype(BF16)
    m_tot = same_block.astype(BF16)
    row_block = lax.broadcasted_iota(jnp.int32, (tc, GLA_DK_HEAD), 0) // GLA_BLOCK

    glr = _pad_rows(glr_ref[...], tc).astype(BF16)
    for hh in range(GLA_HEADS):
        ks = slice(hh * GLA_DK_HEAD, (hh + 1) * GLA_DK_HEAD)
        vs = slice(hh * GLA_DV_HEAD, (hh + 1) * GLA_DV_HEAD)
        q = _pad_rows(qk_ref[:, ks], tc) * (GLA_DK_HEAD ** -0.5)
        k = _pad_rows(qk_ref[:, GLA_DK + hh * GLA_DK_HEAD:GLA_DK + (hh + 1) * GLA_DK_HEAD], tc)
        v = _pad_rows(v_ref[:, vs], tc)
        vb = v.astype(BF16)
        v_t = v.T.astype(BF16)

        pre = _dot(glr, wg_ref[:, ks]) + bg_ref[:, ks]
        log_a = _log_sigmoid(pre) / GLA_TAU
        if t_real < tc:
            log_a = jnp.where(lax.broadcasted_iota(jnp.int32, log_a.shape, 0) < t_real, log_a, 0.0)
        la_hi, la_lo = _split_hi_lo(log_a)
        b = _dot(m_cum, la_hi) + _dot(m_cum, la_lo)
        b_tot = _dot(m_tot, la_hi) + _dot(m_tot, la_lo)
        qe = (q * jnp.exp(b)).astype(BF16)
        ke = (k * jnp.exp(-b)).astype(BF16)
        kd = (k * jnp.exp(b_tot - b)).astype(BF16)
        decay = jnp.exp(b_tot)

        att = jnp.where(causal, _dot_nt(qe, ke), 0.0).astype(BF16)
        o_intra = _dot(att, vb)

        st = st_scr[hh]
        outs = []
        for r in range(n_blocks):
            rows = slice(r * GLA_BLOCK, (r + 1) * GLA_BLOCK)
            outs.append(o_intra[rows] + _dot_nt(qe[rows], st.astype(BF16)))
            kd_r = jnp.where(row_block == r, kd, jnp.zeros_like(kd))
            st = decay[r * GLA_BLOCK:r * GLA_BLOCK + 1, :] * st + _dot(v_t, kd_r)
        st_scr[hh] = st

        o = outs[0] if n_blocks == 1 else jnp.concatenate(outs, axis=0)
        o = o * lax.rsqrt(jnp.mean(o * o, axis=-1, keepdims=True) + RMS_EPS) * nw_ref[...]
        g = gog_ref[:, vs]
        og_ref[:, vs] = (o * (g * jax.nn.sigmoid(g))).astype(BF16)

    @pl.when(t == pl.num_programs(1) - 1)
    def _():
        for hh in range(GLA_HEADS):
            sfin_ref[hh] = st_scr[hh].T


def _gla(proj, glr, w_gate, b_gate, norm_w, s0, batch, seq_len):
    n = batch * seq_len
    tc = min(_GLA_CHUNK, seq_len)
    assert seq_len % tc == 0 and tc % GLA_BLOCK == 0
    steps = seq_len // tc
    rows = lambda b, t: (b * steps + t, 0)
    in_specs = [
        pl.BlockSpec((tc, D_MODEL), lambda b, t: (b * steps + t, 0)),
        pl.BlockSpec((tc, D_MODEL), lambda b, t: (b * steps + t, 1)),
        pl.BlockSpec((tc, D_MODEL), lambda b, t: (b * steps + t, 2)),
        pl.BlockSpec((tc, LANES), rows),
        pl.BlockSpec((LANES, GLA_DK), lambda b, t: (0, 0)),
        pl.BlockSpec((1, GLA_DK), lambda b, t: (0, 0)),
        pl.BlockSpec((1, GLA_DV_HEAD), lambda b, t: (0, 0)),
    ]
    operands = [proj, proj, proj, glr, w_gate, b_gate, norm_w]
    state_spec = pl.BlockSpec((None, GLA_HEADS, GLA_DK_HEAD, GLA_DV_HEAD), lambda b, t: (b, 0, 0, 0))
    if s0 is not None:
        in_specs.append(state_spec)
        operands.append(s0)
    return pl.pallas_call(
        functools.partial(_gla_kernel, t_real=tc, has_s0=s0 is not None),
        grid=(batch, steps),
        in_specs=in_specs,
        out_specs=[pl.BlockSpec((tc, D_MODEL), rows), state_spec],
        out_shape=[
            jax.ShapeDtypeStruct((n, D_MODEL), BF16),
            jax.ShapeDtypeStruct((batch, GLA_HEADS, GLA_DK_HEAD, GLA_DV_HEAD), F32),
        ],
        scratch_shapes=[pltpu.VMEM((GLA_HEADS, GLA_DV_HEAD, GLA_DK_HEAD), F32)],
        compiler_params=_params(("arbitrary", "arbitrary")),
        name="gla_recurrence",
    )(*operands)


_SB_TILE = 128


def _sb_kernel(q_ref, kd_ref, vd_ref, kp_ref, vp_ref, o_ref, *, t_real, past_blocks):
    tq = _SB_TILE
    n_past = pl.program_id(2) if past_blocks is None else past_blocks

    lane = lax.broadcasted_iota(jnp.int32, (tq, LANES), 1)
    head_a = lane < SB_HD
    q = _pad_rows(q_ref[...], tq) * (SB_HD ** -0.5)
    q_a = jnp.where(head_a, q, 0.0).astype(BF16)
    q_b = jnp.where(head_a, 0.0, q).astype(BF16)

    row = lax.broadcasted_iota(jnp.int32, (tq, tq), 0)
    col = lax.broadcasted_iota(jnp.int32, (tq, tq), 1)
    visible = col < row
    suffix = jnp.concatenate([(row > col).astype(BF16), jnp.ones((tq, LANES), BF16)], axis=1)

    def one_head(qh, kb, vb, carry, mask):
        z = _dot_nt(qh, kb)
        soft = jnp.log1p(jnp.exp(-jnp.abs(z)))
        log_rest = -jnp.maximum(z, 0.0) - soft
        log_beta = jnp.minimum(z, 0.0) - soft
        if mask is not None:
            log_rest = jnp.where(mask, log_rest, 0.0)
        hi, lo = _split_hi_lo(log_rest)
        sums = _dot(hi, suffix) + _dot(lo, suffix)
        after = carry + sums[:, :tq]
        a = jnp.exp(log_beta + after)
        if mask is not None:
            a = jnp.where(mask, a, 0.0)
        return _dot(a.astype(BF16), vb), carry + sums[:, tq:]

    def block(kb, vb, state, mask):
        c_a, c_b, acc = state
        pv_a, c_a = one_head(q_a, kb, vb, c_a, mask)
        pv_b, c_b = one_head(q_b, kb, vb, c_b, mask)
        return c_a, c_b, acc + jnp.where(head_a, pv_a, pv_b)

    zero = jnp.zeros((tq, LANES), F32)
    state = block(_pad_rows(kd_ref[...], tq).astype(BF16), _pad_rows(vd_ref[...], tq).astype(BF16),
                  (zero, zero, zero), visible)

    def past(i, state):
        start = pl.multiple_of((n_past - 1 - i) * tq, tq)
        kb = kp_ref[pl.ds(start, tq), :].astype(BF16)
        vb = vp_ref[pl.ds(start, tq), :].astype(BF16)
        return block(kb, vb, state, None)

    state = lax.fori_loop(0, n_past, past, state)
    o_ref[...] = state[2][:t_real].astype(BF16)


def _stick_breaking(proj, sbk, sbv, k_past, v_past, batch, seq_len, past_len):
    n = batch * seq_len
    tq = min(_SB_TILE, seq_len)
    assert seq_len % tq == 0
    q_tiles = seq_len // tq
    sq_col0 = 3 * SB_PAIRS
    own = k_past is None
    if own:
        k_past, v_past, past_len = sbk, sbv, seq_len
    assert past_len % _SB_TILE == 0
    tile = lambda b, p, i: (b * q_tiles + i, p)
    whole = lambda b, p, i: (b, p)
    return pl.pallas_call(
        functools.partial(_sb_kernel, t_real=tq, past_blocks=None if own else past_len // _SB_TILE),
        grid=(batch, SB_PAIRS, q_tiles),
        in_specs=[
            pl.BlockSpec((tq, LANES), lambda b, p, i: (b * q_tiles + i, sq_col0 + p)),
            pl.BlockSpec((tq, LANES), tile),
            pl.BlockSpec((tq, LANES), tile),
            pl.BlockSpec((past_len, LANES), whole),
            pl.BlockSpec((past_len, LANES), whole),
        ],
        out_specs=pl.BlockSpec((tq, LANES), tile),
        out_shape=jax.ShapeDtypeStruct((n, D_MODEL), BF16),
        compiler_params=_params(("arbitrary", "arbitrary", "arbitrary")),
        name="stick_breaking",
    )(proj, sbk, sbv, k_past, v_past)


def _merge_kernel(og_ref, osb_ref, mga_ref, mgb_ref, x_ref, g1_ref, sh2_ref, sc2_ref,
                  wa_ref, wb_ref, wo_ref, l1w_ref, l1b_ref, x1_ref, h2_ref):
    br_a = _dot(og_ref[...], wa_ref[...])
    br_b = _dot(osb_ref[...], wb_ref[...])
    mixed = jax.nn.sigmoid(mga_ref[...]) * br_a + jax.nn.sigmoid(mgb_ref[...]) * br_b
    mix = _dot(mixed.astype(BF16), wo_ref[...])
    x1 = _layer_norm(DN_ALPHA * x_ref[...] + g1_ref[...] * mix) * l1w_ref[...] + l1b_ref[...]
    x1_ref[...] = x1
    h2_ref[...] = (_layer_norm(x1) * (1.0 + sc2_ref[...]) + sh2_ref[...]).astype(BF16)


def _merge(og, osb, proj, x2d, g1, sh2, sc2, per_row, seq_len, wa, wb, wo, l1w, l1b):
    n = x2d.shape[0]
    tm = min(512, seq_len) if not per_row else min(512, n)
    tiles_per_seq = max(seq_len // tm, 1)
    mods = [_mod_operand(v, per_row, tm, tiles_per_seq, 1) for v in (g1, sh2, sc2)]
    rows = lambda i: (i, 0)
    full = lambda i: (0, 0)
    wspec = pl.BlockSpec((D_MODEL, D_MODEL), full)
    vspec = pl.BlockSpec((1, D_MODEL), full)
    return pl.pallas_call(
        _merge_kernel,
        grid=(n // tm,),
        in_specs=[
            pl.BlockSpec((tm, D_MODEL), rows),
            pl.BlockSpec((tm, D_MODEL), rows),
            pl.BlockSpec((tm, D_MODEL), lambda i: (i, 4)),
            pl.BlockSpec((tm, D_MODEL), lambda i: (i, 5)),
            pl.BlockSpec((tm, D_MODEL), rows),
            mods[0][1], mods[1][1], mods[2][1],
            wspec, wspec, wspec, vspec, vspec,
        ],
        out_specs=[pl.BlockSpec((tm, D_MODEL), rows), pl.BlockSpec((tm, D_MODEL), rows)],
        out_shape=[jax.ShapeDtypeStruct((n, D_MODEL), F32), jax.ShapeDtypeStruct((n, D_MODEL), BF16)],
        compiler_params=_params(("arbitrary",)),
        name="merge_ln1",
    )(og, osb, proj, proj, x2d, mods[0][0], mods[1][0], mods[2][0], wa, wb, wo, l1w, l1b)


def _extract_top(s, count):
    m = s.shape[0]
    idx = lax.broadcasted_iota(jnp.int32, s.shape, 0).astype(F32)
    rank = lax.broadcasted_iota(jnp.int32, (count, s.shape[1]), 0)

    def step(r, carry):
        cur, taken, vals = carry
        top = jnp.max(cur, axis=0, keepdims=True)
        first = jnp.min(jnp.where(cur == top, idx, float(m)), axis=0, keepdims=True)
        hit = idx == first
        vals = jnp.where(rank == r, top, vals)
        return jnp.where(hit, NEG_INF, cur), jnp.where(hit, 1.0, taken), vals

    init = (s, jnp.zeros(s.shape, F32), jnp.zeros((count, s.shape[1]), F32))
    _, taken, vals = lax.fori_loop(0, count, step, init)
    return vals, taken > 0.5


def _route_kernel(h2_ref, wq_ref, keys_ref, s1m_ref, c1_ref, s2m_ref, e2_ref, thr_ref):
    q = _dot(h2_ref[...], wq_ref[...]).astype(BF16)
    s1 = _dot_nt(keys_ref[0], q[:, :PEER_HALF])
    s2 = _dot_nt(keys_ref[1], q[:, PEER_HALF:])
    v1, in1 = _extract_top(s1, PEER_TOPK)
    v2, in2 = _extract_top(s2, PEER_TOPK)
    cand = jnp.concatenate([v1[r:r + 1, :] + v2 for r in range(PEER_TOPK)], axis=0)
    top, _ = _extract_top(cand, PEER_TOPK)
    best = top[0:1, :]
    z = jnp.sum(jnp.exp(top - best), axis=0, keepdims=True)
    s1m_ref[...] = jnp.where(in1, s1, NEG_INF)
    s2m_ref[...] = jnp.where(in2, s2, NEG_INF)
    c1_ref[...] = jnp.where(in1, jnp.exp(s1 - v1[0:1, :]), 0.0) / z
    e2_ref[...] = jnp.where(in2, jnp.exp(s2 - v2[0:1, :]), 0.0)
    thr_ref[...] = jnp.broadcast_to(top[PEER_TOPK - 1:PEER_TOPK, :], thr_ref.shape)


def _peer_route(h2, wq, keys):
    n = h2.shape[0]
    tn = min(256, n)
    per_head = pl.BlockSpec((None, N_KEYS, tn), lambda i, h: (h, 0, i))
    shape = jax.ShapeDtypeStruct((PEER_HEADS, N_KEYS, n), F32)
    return pl.pallas_call(
        _route_kernel,
        grid=(n // tn, PEER_HEADS),
        in_specs=[
            pl.BlockSpec((tn, D_MODEL), lambda i, h: (i, 0)),
            pl.BlockSpec((D_MODEL, 2 * PEER_HALF), lambda i, h: (0, h)),
            pl.BlockSpec((None, 2, N_KEYS, PEER_HALF), lambda i, h: (h, 0, 0, 0)),
        ],
        out_specs=[per_head, per_head, per_head, per_head,
                   pl.BlockSpec((None, 8, tn), lambda i, h: (h, 0, i))],
        out_shape=[shape, shape, shape, shape, jax.ShapeDtypeStruct((PEER_HEADS, 8, n), F32)],
        compiler_params=_params(("arbitrary", "arbitrary")),
        name="peer_route",
    )(h2, wq, keys)


_EXPERT_CHUNK = 1024
_ROWS_PER_CHUNK = _EXPERT_CHUNK // N_KEYS


def _gelu_tanh(x):
    return 0.5 * x * (1.0 + jnp.tanh(0.7978845608028654 * (x + 0.044715 * (x * x * x))))


def _expert_kernel(h2_ref, u_ref, vt_ref, s1m_ref, c1_ref, s2m_ref, e2_ref, thr_ref,
                   x1_ref, g2_ref, l2w_ref, l2b_ref, y_ref, acc_scr, act_scr, w_scr):
    c = pl.program_id(1)

    @pl.when(c == 0)
    def _():
        acc_scr[...] = jnp.zeros_like(acc_scr)

    act_scr[...] = _dot_nt(u_ref[...], h2_ref[...])

    def first_key(ii, _):
        r0 = pl.multiple_of(ii * N_KEYS, N_KEYS)
        gate = jnp.zeros((N_KEYS, act_scr.shape[1]), F32)
        for h in range(PEER_HEADS):
            cand = s1m_ref[h, pl.ds(ii, 1), :] + s2m_ref[h]
            weight = e2_ref[h] * c1_ref[h, pl.ds(ii, 1), :]
            gate = gate + jnp.where(cand >= thr_ref[h, 0:1, :], weight, 0.0)
        w_scr[pl.ds(r0, N_KEYS), :] = (gate * _gelu_tanh(act_scr[pl.ds(r0, N_KEYS), :])).astype(BF16)
        return 0

    lax.fori_loop(0, _ROWS_PER_CHUNK, first_key, 0)
    acc_scr[...] += _dot(vt_ref[...], w_scr[...])

    @pl.when(c == pl.num_programs(1) - 1)
    def _():
        f = acc_scr[...].T
        y = _layer_norm(DN_ALPHA * x1_ref[...] + g2_ref[...] * f)
        y_ref[...] = y * l2w_ref[...] + l2b_ref[...]


def _peer_experts(h2, u, v_t, route, x1, g2, per_row, seq_len, l2w, l2b):
    n = h2.shape[0]
    tg = min(512, seq_len) if not per_row else min(512, n)
    tiles_per_seq = max(seq_len // tg, 1)
    g2_arr, g2_spec = _mod_operand(g2, per_row, tg, tiles_per_seq, 2)
    s1m, c1, s2m, e2, thr = route
    rows = lambda i, c: (i, 0)
    full = lambda i, c: (0, 0)
    chunk_rows = pl.BlockSpec((PEER_HEADS, _ROWS_PER_CHUNK, tg), lambda i, c: (0, c, i))
    all_rows = pl.BlockSpec((PEER_HEADS, N_KEYS, tg), lambda i, c: (0, 0, i))
    return pl.pallas_call(
        _expert_kernel,
        grid=(n // tg, N_EXPERTS // _EXPERT_CHUNK),
        in_specs=[
            pl.BlockSpec((tg, D_MODEL), rows),
            pl.BlockSpec((_EXPERT_CHUNK, D_MODEL), lambda i, c: (c, 0)),
            pl.BlockSpec((D_MODEL, _EXPERT_CHUNK), lambda i, c: (0, c)),
            chunk_rows, chunk_rows, all_rows, all_rows,
            pl.BlockSpec((PEER_HEADS, 8, tg), lambda i, c: (0, 0, i)),
            pl.BlockSpec((tg, D_MODEL), rows),
            g2_spec,
            pl.BlockSpec((1, D_MODEL), full),
            pl.BlockSpec((1, D_MODEL), full),
        ],
        out_specs=pl.BlockSpec((tg, D_MODEL), rows),
        out_shape=jax.ShapeDtypeStruct((n, D_MODEL), F32),
        scratch_shapes=[
            pltpu.VMEM((D_MODEL, tg), F32),
            pltpu.VMEM((_EXPERT_CHUNK, tg), F32),
            pltpu.VMEM((_EXPERT_CHUNK, tg), BF16),
        ],
        compiler_params=_params(("arbitrary", "arbitrary")),
        name="peer_experts_ln2",
    )(h2, u, v_t, s1m, c1, s2m, e2, thr, x1, g2_arr, l2w, l2b)


def _trunk_layer(x, mod, s0, k_past, v_past, wts):
    batch, seq_len, _ = x.shape
    n = batch * seq_len
    per_row = seq_len < 128
    if per_row:
        mod = jnp.repeat(mod, seq_len, axis=0)
    sh1, sc1, g1, sh2, sc2, g2 = [mod[:, k * D_MODEL:(k + 1) * D_MODEL] for k in range(N_MOD)]
    x2d = x.reshape(n, D_MODEL)

    proj, sbk, sbv, glr = _input_projection(x2d, sh1, sc1, per_row, seq_len, wts["w_main"], wts["w_glr"])
    og, s_fin = _gla(proj, glr, wts["w_gate"], wts["b_gate"], wts["norm_w"], s0, batch, seq_len)
    if k_past is None:
        osb = _stick_breaking(proj, sbk, sbv, None, None, batch, seq_len, 0)
    else:
        past_len = k_past.shape[1]
        osb = _stick_breaking(proj, sbk, sbv, k_past.reshape(batch * past_len, D_MODEL),
                              v_past.reshape(batch * past_len, D_MODEL), batch, seq_len, past_len)
    x1, h2 = _merge(og, osb, proj, x2d, g1, sh2, sc2, per_row, seq_len,
                    wts["w_a"], wts["w_b"], wts["w_o"], wts["ln1_w"], wts["ln1_b"])
    route = _peer_route(h2, wts["peer_wq"], wts["peer_keys"])
    y = _peer_experts(h2, wts["peer_u"], wts["peer_vt"], route, x1, g2, per_row, seq_len,
                      wts["ln2_w"], wts["ln2_b"])
    shape_kv = (batch, seq_len, SB_HEADS, SB_HD)
    return y.reshape(batch, seq_len, D_MODEL), s_fin, sbk.reshape(shape_kv), sbv.reshape(shape_kv)


def kernel(x_prompt, x_sample, state_gla, cache_sb_k, cache_sb_v, c_prompt, c_sample, w_ada, b_ada, w_in, w_gla_gate, b_gla_gate, gla_norm_w, w_gla_proj, w_sb_proj, w_o, ln1_w, ln1_b, peer_wq, peer_keys, peer_u, peer_v, ln2_w, ln2_b):
    assert w_in.shape[0] == DEPTH
    l = 0
    glr0 = 2 * GLA_DK + GLA_HEADS * GLA_DV_HEAD
    w = w_in[l]
    wts = {
        "w_main": jnp.concatenate([w[:, :glr0], w[:, glr0 + GLA_GATE_RANK:]], axis=1).astype(BF16),
        "w_glr": jnp.pad(w[:, glr0:glr0 + GLA_GATE_RANK], ((0, 0), (0, LANES - GLA_GATE_RANK))).astype(BF16),
        "w_gate": jnp.pad(w_gla_gate[l], ((0, LANES - GLA_GATE_RANK), (0, 0))).astype(BF16),
        "b_gate": b_gla_gate[l][None, :],
        "norm_w": gla_norm_w[l][None, :],
        "w_a": w_gla_proj[l].astype(BF16),
        "w_b": w_sb_proj[l].astype(BF16),
        "w_o": w_o[l].astype(BF16),
        "ln1_w": ln1_w[l][None, :],
        "ln1_b": ln1_b[l][None, :],
        "peer_wq": peer_wq[l].astype(BF16),
        "peer_keys": peer_keys[l].astype(BF16),
        "peer_u": peer_u[l].astype(BF16),
        "peer_vt": peer_v[l].astype(BF16).T,
        "ln2_w": ln2_w[l][None, :],
        "ln2_b": ln2_b[l][None, :],
    }
    n_prompt = c_prompt.shape[0]
    mod = _modulation(jnp.concatenate([c_prompt, c_sample], axis=0), w_ada[l], b_ada[l][None, :])

    yp, sp, kp, vp = _trunk_layer(x_prompt, mod[:n_prompt], None, None, None, wts)
    ys, ss, ksm, vsm = _trunk_layer(x_sample, mod[n_prompt:], state_gla[l], cache_sb_k[l], cache_sb_v[l], wts)
    return (yp, ys, sp[None], kp[None], vp[None], ss[None], ksm[None], vsm[None])
```

```python
import functools

import jax
import jax.numpy as jnp
from jax import lax
from jax.experimental import pallas as pl
from jax.experimental.pallas import tpu as pltpu

F32 = jnp.float32
BF16 = jnp.bfloat16

D_MODEL = 1024
N_MOD = 6
GLA_HEADS = 4
GLA_DK_HEAD = 128
GLA_DV_HEAD = 256
GLA_DK = GLA_HEADS * GLA_DK_HEAD
GLA_GATE_RANK = 16
GLA_TAU = 16.0
GLA_BLOCK = 16
SB_HEADS = 16
SB_HD = 64
SB_PAIRS = SB_HEADS // 2
PEER_HEADS = 8
PEER_HALF = 128
N_KEYS = 128
N_EXPERTS = N_KEYS * N_KEYS
PEER_TOPK = 16
DEPTH = 1
DN_ALPHA = (2.0 * DEPTH) ** 0.25
LN_EPS = 1e-5
RMS_EPS = 1e-6

LANES = 128
VMEM_LIMIT = 48 * 1024 * 1024
NEG_INF = float("-inf")

_NT = (((1,), (1,)), ((), ()))


def _dot(a, b):
    return jnp.dot(a, b, preferred_element_type=F32)


def _dot_nt(a, b):
    return lax.dot_general(a, b, _NT, preferred_element_type=F32)


def _layer_norm(x):
    mu = jnp.mean(x, axis=-1, keepdims=True)
    xc = x - mu
    var = jnp.mean(xc * xc, axis=-1, keepdims=True)
    return xc * lax.rsqrt(var + LN_EPS)


def _log_sigmoid(x):
    return jnp.minimum(x, 0.0) - jnp.log1p(jnp.exp(-jnp.abs(x)))


def _split_hi_lo(x):
    hi = x.astype(BF16)
    lo = (x - hi.astype(F32)).astype(BF16)
    return hi, lo


def _params(semantics):
    return pltpu.CompilerParams(dimension_semantics=semantics, vmem_limit_bytes=VMEM_LIMIT)


def _mod_operand(vec, per_row, tile, tiles_per_seq, grid_rank):
    pad = (0,) * (grid_rank - 1)
    if per_row:
        return vec, pl.BlockSpec((tile, D_MODEL), lambda i, *_: (i, 0))
    arr = vec[:, None, :]
    return arr, pl.BlockSpec((None, 1, D_MODEL), lambda i, *_: (i // tiles_per_seq, 0, 0))


def _mod_kernel(c_ref, w_ref, b_ref, o_ref):
    c = c_ref[...]
    s = (c * jax.nn.sigmoid(c)).astype(BF16)
    o_ref[...] = _dot(s, w_ref[...].astype(BF16)) + b_ref[...]


def _modulation(c_all, w_ada, b_ada):
    rows = c_all.shape[0]
    return pl.pallas_call(
        _mod_kernel,
        grid=(N_MOD,),
        in_specs=[
            pl.BlockSpec((rows, D_MODEL), lambda j: (0, 0)),
            pl.BlockSpec((D_MODEL, D_MODEL), lambda j: (0, j)),
            pl.BlockSpec((1, D_MODEL), lambda j: (0, j)),
        ],
        out_specs=pl.BlockSpec((rows, D_MODEL), lambda j: (0, j)),
        out_shape=jax.ShapeDtypeStruct((rows, N_MOD * D_MODEL), F32),
        compiler_params=_params(("arbitrary",)),
        name="adaln_mod",
    )(c_all, w_ada, b_ada)


_N_COL_TILES = 8
_COL_SK = 4
_COL_SV = 5


def _inproj_kernel(x_ref, sh_ref, sc_ref, w_ref, wglr_ref, proj_ref, sbk_ref, sbv_ref, glr_ref, h_scr):
    j = pl.program_id(1)

    @pl.when(j == 0)
    def _():
        h = _layer_norm(x_ref[...]) * (1.0 + sc_ref[...]) + sh_ref[...]
        hb = h.astype(BF16)
        h_scr[...] = hb
        glr_ref[...] = _dot(hb, wglr_ref[...])

    acc = _dot(h_scr[...], w_ref[...])

    @pl.when(j == _COL_SK)
    def _():
        sbk_ref[...] = acc

    @pl.when(j == _COL_SV)
    def _():
        sbv_ref[...] = acc

    @pl.when((j < _COL_SK) | (j > _COL_SV))
    def _():
        proj_ref[...] = acc


def _proj_col(j):
    return jnp.where(j < _COL_SK, j, jnp.where(j <= _COL_SV, _COL_SK - 1, j - 2))


def _input_projection(x2d, shift, scale, per_row, seq_len, w_main, w_glr):
    n = x2d.shape[0]
    tm = min(1024, seq_len) if not per_row else min(512, n)
    tiles_per_seq = max(seq_len // tm, 1)
    sh_arr, sh_spec = _mod_operand(shift, per_row, tm, tiles_per_seq, 2)
    sc_arr, sc_spec = _mod_operand(scale, per_row, tm, tiles_per_seq, 2)
    row_out = lambda i, j: (i, 0)
    return pl.pallas_call(
        _inproj_kernel,
        grid=(n // tm, _N_COL_TILES),
        in_specs=[
            pl.BlockSpec((tm, D_MODEL), row_out),
            sh_spec,
            sc_spec,
            pl.BlockSpec((D_MODEL, D_MODEL), lambda i, j: (0, j)),
            pl.BlockSpec((D_MODEL, LANES), lambda i, j: (0, 0)),
        ],
        out_specs=[
            pl.BlockSpec((tm, D_MODEL), lambda i, j: (i, _proj_col(j))),
            pl.BlockSpec((tm, D_MODEL), row_out),
            pl.BlockSpec((tm, D_MODEL), row_out),
            pl.BlockSpec((tm, LANES), row_out),
        ],
        out_shape=[
            jax.ShapeDtypeStruct((n, 6 * D_MODEL), F32),
            jax.ShapeDtypeStruct((n, D_MODEL), F32),
            jax.ShapeDtypeStruct((n, D_MODEL), F32),
            jax.ShapeDtypeStruct((n, LANES), F32),
        ],
        scratch_shapes=[pltpu.VMEM((tm, D_MODEL), BF16)],
        compiler_params=_params(("arbitrary", "arbitrary")),
        name="ln_inproj",
    )(x2d, sh_arr, sc_arr, w_main, w_glr)


_GLA_CHUNK = 128


def _pad_rows(x, rows):
    if x.shape[0] == rows:
        return x
    return jnp.concatenate([x, jnp.zeros((rows - x.shape[0], x.shape[1]), x.dtype)], axis=0)


def _gla_kernel(*refs, t_real, has_s0):
    if has_s0:
        (qk_ref, v_ref, gog_ref, glr_ref, wg_ref, bg_ref, nw_ref, s0_ref, og_ref, sfin_ref, st_scr) = refs
    else:
        (qk_ref, v_ref, gog_ref, glr_ref, wg_ref, bg_ref, nw_ref, og_ref, sfin_ref, st_scr) = refs
        s0_ref = None
    t = pl.program_id(1)
    tc = _GLA_CHUNK
    n_blocks = t_real // GLA_BLOCK

    @pl.when(t == 0)
    def _():
        for hh in range(GLA_HEADS):
            if has_s0:
                st_scr[hh] = s0_ref[hh].T
            else:
                st_scr[hh] = jnp.zeros((GLA_DV_HEAD, GLA_DK_HEAD), F32)

    row = lax.broadcasted_iota(jnp.int32, (tc, tc), 0)
    col = lax.broadcasted_iota(jnp.int32, (tc, tc), 1)
    same_block = (row // GLA_BLOCK) == (col // GLA_BLOCK)
    causal = same_block & (col <= row)
    m_cum = causal.astype(BF16)
    m_tot = same_block.astype(BF16)
    row_block = lax.broadcasted_iota(jnp.int32, (tc, GLA_DK_HEAD), 0) // GLA_BLOCK

    glr = _pad_rows(glr_ref[...], tc).astype(BF16)
    for hh in range(GLA_HEADS):
        ks = slice(hh * GLA_DK_HEAD, (hh + 1) * GLA_DK_HEAD)
        vs = slice(hh * GLA_DV_HEAD, (hh + 1) * GLA_DV_HEAD)
        q = _pad_rows(qk_ref[:, ks], tc) * (GLA_DK_HEAD ** -0.5)
        k = _pad_rows(qk_ref[:, GLA_DK + hh * GLA_DK_HEAD:GLA_DK + (hh + 1) * GLA_DK_HEAD], tc)
        v = _pad_rows(v_ref[:, vs], tc)
        vb = v.astype(BF16)
        v_t = v.T.astype(BF16)

        pre = _dot(glr, wg_ref[:, ks]) + bg_ref[:, ks]
        log_a = _log_sigmoid(pre) / GLA_TAU
        if t_real < tc:
            log_a = jnp.where(lax.broadcasted_iota(jnp.int32, log_a.shape, 0) < t_real, log_a, 0.0)
        la_hi, la_lo = _split_hi_lo(log_a)
        b = _dot(m_cum, la_hi) + _dot(m_cum, la_lo)
        b_tot = _dot(m_tot, la_hi) + _dot(m_tot, la_lo)
        qe = (q * jnp.exp(b)).astype(BF16)
        ke = (k * jnp.exp(-b)).astype(BF16)
        kd = (k * jnp.exp(b_tot - b)).astype(BF16)
        decay = jnp.exp(b_tot)

        att = jnp.where(causal, _dot_nt(qe, ke), 0.0).astype(BF16)
        o_intra = _dot(att, vb)

        st = st_scr[hh]
        outs = []
        for r in range(n_blocks):
            rows = slice(r * GLA_BLOCK, (r + 1) * GLA_BLOCK)
            outs.append(o_intra[rows] + _dot_nt(qe[rows], st.astype(BF16)))
            kd_r = jnp.where(row_block == r, kd, jnp.zeros_like(kd))
            st = decay[r * GLA_BLOCK:r * GLA_BLOCK + 1, :] * st + _dot(v_t, kd_r)
        st_scr[hh] = st

        o = outs[0] if n_blocks == 1 else jnp.concatenate(outs, axis=0)
        o = o * lax.rsqrt(jnp.mean(o * o, axis=-1, keepdims=True) + RMS_EPS) * nw_ref[...]
        g = gog_ref[:, vs]
        og_ref[:, vs] = (o * (g * jax.nn.sigmoid(g))).astype(BF16)

    @pl.when(t == pl.num_programs(1) - 1)
    def _():
        for hh in range(GLA_HEADS):
            sfin_ref[hh] = st_scr[hh].T


def _gla(proj, glr, w_gate, b_gate, norm_w, s0, batch, seq_len):
    n = batch * seq_len
    tc = min(_GLA_CHUNK, seq_len)
    assert seq_len % tc == 0 and tc % GLA_BLOCK == 0
    steps = seq_len // tc
    rows = lambda b, t: (b * steps + t, 0)
    in_specs = [
        pl.BlockSpec((tc, D_MODEL), lambda b, t: (b * steps + t, 0)),
        pl.BlockSpec((tc, D_MODEL), lambda b, t: (b * steps + t, 1)),
        pl.BlockSpec((tc, D_MODEL), lambda b, t: (b * steps + t, 2)),
        pl.BlockSpec((tc, LANES), rows),
        pl.BlockSpec((LANES, GLA_DK), lambda b, t: (0, 0)),
        pl.BlockSpec((1, GLA_DK), lambda b, t: (0, 0)),
        pl.BlockSpec((1, GLA_DV_HEAD), lambda b, t: (0, 0)),
    ]
    operands = [proj, proj, proj, glr, w_gate, b_gate, norm_w]
    state_spec = pl.BlockSpec((None, GLA_HEADS, GLA_DK_HEAD, GLA_DV_HEAD), lambda b, t: (b, 0, 0, 0))
    if s0 is not None:
        in_specs.append(state_spec)
        operands.append(s0)
    return pl.pallas_call(
        functools.partial(_gla_kernel, t_real=tc, has_s0=s0 is not None),
        grid=(batch, steps),
        in_specs=in_specs,
        out_specs=[pl.BlockSpec((tc, D_MODEL), rows), state_spec],
        out_shape=[
            jax.ShapeDtypeStruct((n, D_MODEL), BF16),
            jax.ShapeDtypeStruct((batch, GLA_HEADS, GLA_DK_HEAD, GLA_DV_HEAD), F32),
        ],
        scratch_shapes=[pltpu.VMEM((GLA_HEADS, GLA_DV_HEAD, GLA_DK_HEAD), F32)],
        compiler_params=_params(("arbitrary", "arbitrary")),
        name="gla_recurrence",
    )(*operands)


_SB_TILE = 128
_SB_DEAD = -88.0


def _sb_kernel(q_ref, kd_ref, vd_ref, kp_ref, vp_ref, o_ref, *, t_real, past_blocks):
    tq = _SB_TILE
    n_past = pl.program_id(2) if past_blocks is None else past_blocks

    lane = lax.broadcasted_iota(jnp.int32, (tq, LANES), 1)
    head_a = lane < SB_HD
    q = _pad_rows(q_ref[...], tq) * (SB_HD ** -0.5)
    q_a = jnp.where(head_a, q, 0.0).astype(BF16)
    q_b = jnp.where(head_a, 0.0, q).astype(BF16)

    r2 = lax.broadcasted_iota(jnp.int32, (2 * tq, 2 * tq), 0)
    c2 = lax.broadcasted_iota(jnp.int32, (2 * tq, 2 * tq), 1)
    suffix2 = jnp.concatenate([(r2 > c2).astype(BF16), jnp.ones((2 * tq, LANES), BF16)], axis=1)
    suffix1 = suffix2[tq:, tq:]

    def one_head(qh, kb, vb, carry, mask, suffix):
        width = kb.shape[0]
        z = _dot_nt(qh, kb)
        soft = jnp.log1p(jnp.exp(-jnp.abs(z)))
        log_rest = -jnp.maximum(z, 0.0) - soft
        log_beta = jnp.minimum(z, 0.0) - soft
        if mask is not None:
            log_rest = jnp.where(mask, log_rest, 0.0)
        hi, lo = _split_hi_lo(log_rest)
        sums = _dot(hi, suffix) + _dot(lo, suffix)
        after = sums[:, :width]
        if carry is not None:
            after = after + carry
        a = jnp.exp(log_beta + after)
        if mask is not None:
            a = jnp.where(mask, a, 0.0)
        total = sums[:, width:]
        return _dot(a.astype(BF16), vb), total if carry is None else carry + total

    def live(c_a, c_b):
        return (jnp.max(jnp.maximum(c_a, c_b)) > _SB_DEAD).astype(jnp.int32)

    row = lax.broadcasted_iota(jnp.int32, (tq, 2 * tq), 0)
    col = lax.broadcasted_iota(jnp.int32, (tq, 2 * tq), 1)
    prev_cols = jnp.where(n_past > 0, tq, 0)
    visible = (col < prev_cols) | ((col >= tq) & (col - tq < row))
    prev_start = pl.multiple_of(jnp.maximum(n_past - 1, 0) * tq, tq)
    k2 = jnp.concatenate([kp_ref[pl.ds(prev_start, tq), :], _pad_rows(kd_ref[...], tq)], axis=0).astype(BF16)
    v2 = jnp.concatenate([vp_ref[pl.ds(prev_start, tq), :], _pad_rows(vd_ref[...], tq)], axis=0).astype(BF16)
    pv_a, c_a = one_head(q_a, k2, v2, None, visible, suffix2)
    pv_b, c_b = one_head(q_b, k2, v2, None, visible, suffix2)
    acc = jnp.where(head_a, pv_a, pv_b)

    def more(state):
        j, alive, _, _, _ = state
        return (j >= 0) & (alive > 0)

    def older(state):
        j, _, c_a, c_b, acc = state
        start = pl.multiple_of(j * tq, tq)
        kb = kp_ref[pl.ds(start, tq), :].astype(BF16)
        vb = vp_ref[pl.ds(start, tq), :].astype(BF16)
        pv_a, c_a = one_head(q_a, kb, vb, c_a, None, suffix1)
        pv_b, c_b = one_head(q_b, kb, vb, c_b, None, suffix1)
        return j - 1, live(c_a, c_b), c_a, c_b, acc + jnp.where(head_a, pv_a, pv_b)

    state = lax.while_loop(more, older, (n_past - 2, live(c_a, c_b), c_a, c_b, acc))
    o_ref[...] = state[4][:t_real].astype(BF16)


def _stick_breaking(proj, sbk, sbv, k_past, v_past, batch, seq_len, past_len):
    n = batch * seq_len
    tq = min(_SB_TILE, seq_len)
    assert seq_len % tq == 0
    q_tiles = seq_len // tq
    sq_col0 = 3 * SB_PAIRS
    own = k_past is None
    if own:
        k_past, v_past, past_len = sbk, sbv, seq_len
    assert past_len % _SB_TILE == 0
    tile = lambda b, p, i: (b * q_tiles + i, p)
    whole = lambda b, p, i: (b, p)
    return pl.pallas_call(
        functools.partial(_sb_kernel, t_real=tq, past_blocks=None if own else past_len // _SB_TILE),
        grid=(batch, SB_PAIRS, q_tiles),
        in_specs=[
            pl.BlockSpec((tq, LANES), lambda b, p, i: (b * q_tiles + i, sq_col0 + p)),
            pl.BlockSpec((tq, LANES), tile),
            pl.BlockSpec((tq, LANES), tile),
            pl.BlockSpec((past_len, LANES), whole),
            pl.BlockSpec((past_len, LANES), whole),
        ],
        out_specs=pl.BlockSpec((tq, LANES), tile),
        out_shape=jax.ShapeDtypeStruct((n, D_MODEL), BF16),
        compiler_params=_params(("arbitrary", "arbitrary", "arbitrary")),
        name="stick_breaking",
    )(proj, sbk, sbv, k_past, v_past)


def _merge_kernel(og_ref, osb_ref, mga_ref, mgb_ref, x_ref, g1_ref, sh2_ref, sc2_ref,
                  wa_ref, wb_ref, wo_ref, l1w_ref, l1b_ref, x1_ref, h2_ref):
    br_a = _dot(og_ref[...], wa_ref[...])
    br_b = _dot(osb_ref[...], wb_ref[...])
    mixed = jax.nn.sigmoid(mga_ref[...]) * br_a + jax.nn.sigmoid(mgb_ref[...]) * br_b
    mix = _dot(mixed.astype(BF16), wo_ref[...])
    x1 = _layer_norm(DN_ALPHA * x_ref[...] + g1_ref[...] * mix) * l1w_ref[...] + l1b_ref[...]
    x1_ref[...] = x1
    h2_ref[...] = (_layer_norm(x1) * (1.0 + sc2_ref[...]) + sh2_ref[...]).astype(BF16)


def _merge(og, osb, proj, x2d, g1, sh2, sc2, per_row, seq_len, wa, wb, wo, l1w, l1b):
    n = x2d.shape[0]
    tm = min(512, seq_len) if not per_row else min(512, n)
    tiles_per_seq = max(seq_len // tm, 1)
    mods = [_mod_operand(v, per_row, tm, tiles_per_seq, 1) for v in (g1, sh2, sc2)]
    rows = lambda i: (i, 0)
    full = lambda i: (0, 0)
    wspec = pl.BlockSpec((D_MODEL, D_MODEL), full)
    vspec = pl.BlockSpec((1, D_MODEL), full)
    return pl.pallas_call(
        _merge_kernel,
        grid=(n // tm,),
        in_specs=[
            pl.BlockSpec((tm, D_MODEL), rows),
            pl.BlockSpec((tm, D_MODEL), rows),
            pl.BlockSpec((tm, D_MODEL), lambda i: (i, 4)),
            pl.BlockSpec((tm, D_MODEL), lambda i: (i, 5)),
            pl.BlockSpec((tm, D_MODEL), rows),
            mods[0][1], mods[1][1], mods[2][1],
            wspec, wspec, wspec, vspec, vspec,
        ],
        out_specs=[pl.BlockSpec((tm, D_MODEL), rows), pl.BlockSpec((tm, D_MODEL), rows)],
        out_shape=[jax.ShapeDtypeStruct((n, D_MODEL), F32), jax.ShapeDtypeStruct((n, D_MODEL), BF16)],
        compiler_params=_params(("arbitrary",)),
        name="merge_ln1",
    )(og, osb, proj, proj, x2d, mods[0][0], mods[1][0], mods[2][0], wa, wb, wo, l1w, l1b)


def _extract_top(s, count):
    m = s.shape[0]
    idx = lax.broadcasted_iota(jnp.int32, s.shape, 0).astype(F32)
    rank = lax.broadcasted_iota(jnp.int32, (count, s.shape[1]), 0)

    def step(r, carry):
        cur, taken, vals = carry
        top = jnp.max(cur, axis=0, keepdims=True)
        first = jnp.min(jnp.where(cur == top, idx, float(m)), axis=0, keepdims=True)
        hit = idx == first
        vals = jnp.where(rank == r, top, vals)
        return jnp.where(hit, NEG_INF, cur), jnp.where(hit, 1.0, taken), vals

    init = (s, jnp.zeros(s.shape, F32), jnp.zeros((count, s.shape[1]), F32))
    _, taken, vals = lax.fori_loop(0, count, step, init)
    return vals, taken > 0.5


def _pair_candidates(v1, v2):
    k = PEER_TOPK
    row8 = lax.broadcasted_iota(jnp.int32, (8, v1.shape[1]), 0)
    groups = [v1[0:1, :] + v2]
    for a in range(1, 8):
        groups.append(jnp.where(row8 < k // (a + 1), v1[a:a + 1, :] + v2[0:8, :], NEG_INF))
    groups.append(v1[8:k, :] + v2[0:1, :])
    return jnp.concatenate(groups, axis=0)


def _route_kernel(h2_ref, wq_ref, keys_ref, s1m_ref, c1_ref, s2m_ref, e2_ref, thr_ref):
    q = _dot(h2_ref[...], wq_ref[...]).astype(BF16)
    s1 = _dot_nt(keys_ref[0], q[:, :PEER_HALF])
    s2 = _dot_nt(keys_ref[1], q[:, PEER_HALF:])
    v1, in1 = _extract_top(s1, PEER_TOPK)
    v2, in2 = _extract_top(s2, PEER_TOPK)
    top, _ = _extract_top(_pair_candidates(v1, v2), PEER_TOPK)
    best = top[0:1, :]
    z = jnp.sum(jnp.exp(top - best), axis=0, keepdims=True)
    s1m_ref[...] = jnp.where(in1, s1, NEG_INF)
    s2m_ref[...] = jnp.where(in2, s2, NEG_INF)
    c1_ref[...] = jnp.where(in1, jnp.exp(s1 - v1[0:1, :]), 0.0) / z
    e2_ref[...] = jnp.where(in2, jnp.exp(s2 - v2[0:1, :]), 0.0)
    thr_ref[...] = jnp.broadcast_to(top[PEER_TOPK - 1:PEER_TOPK, :], thr_ref.shape)


def _peer_route(h2, wq, keys):
    n = h2.shape[0]
    tn = min(256, n)
    per_head = pl.BlockSpec((None, N_KEYS, tn), lambda i, h: (h, 0, i))
    shape = jax.ShapeDtypeStruct((PEER_HEADS, N_KEYS, n), F32)
    return pl.pallas_call(
        _route_kernel,
        grid=(n // tn, PEER_HEADS),
        in_specs=[
            pl.BlockSpec((tn, D_MODEL), lambda i, h: (i, 0)),
            pl.BlockSpec((D_MODEL, 2 * PEER_HALF), lambda i, h: (0, h)),
            pl.BlockSpec((None, 2, N_KEYS, PEER_HALF), lambda i, h: (h, 0, 0, 0)),
        ],
        out_specs=[per_head, per_head, per_head, per_head,
                   pl.BlockSpec((None, 8, tn), lambda i, h: (h, 0, i))],
        out_shape=[shape, shape, shape, shape, jax.ShapeDtypeStruct((PEER_HEADS, 8, n), F32)],
        compiler_params=_params(("arbitrary", "arbitrary")),
        name="peer_route",
    )(h2, wq, keys)


_EXPERT_CHUNK = 1024
_ROWS_PER_CHUNK = _EXPERT_CHUNK // N_KEYS
_EXPERT_SUB = 256
_SUBS_PER_CHUNK = _EXPERT_CHUNK // _EXPERT_SUB


def _gelu_tanh(x):
    return 0.5 * x * (1.0 + jnp.tanh(0.7978845608028654 * (x + 0.044715 * (x * x * x))))


def _expert_kernel(h2_ref, u_ref, vt_ref, s1m_ref, c1_ref, s2m_ref, e2_ref, thr_ref,
                   x1_ref, g2_ref, l2w_ref, l2b_ref, y_ref, acc_scr, w_scr):
    c = pl.program_id(1)
    n_chunks = pl.num_programs(1) - 1
    tokens = w_scr.shape[1]
    keys_per_sub = _EXPERT_SUB // N_KEYS

    @pl.when(c == 0)
    def _():
        acc_scr[...] = jnp.zeros_like(acc_scr)
        w_scr[...] = jnp.zeros_like(w_scr)

    def down(k):
        e0 = pl.multiple_of(k * _EXPERT_SUB, _EXPERT_SUB)
        acc_scr[...] += _dot(vt_ref[k], w_scr[pl.ds(e0, _EXPERT_SUB), :])

    @pl.when(c < n_chunks)
    def _():
        def sub(k, _):
            down(k)
            e0 = pl.multiple_of(k * _EXPERT_SUB, _EXPERT_SUB)
            act = _dot_nt(u_ref[pl.ds(e0, _EXPERT_SUB), :], h2_ref[...])
            for half in range(keys_per_sub):
                ii = k * keys_per_sub + half
                gate = jnp.zeros((N_KEYS, tokens), F32)
                for h in range(PEER_HEADS):
                    cand = s1m_ref[h, pl.ds(ii, 1), :] + s2m_ref[h]
                    weight = e2_ref[h] * c1_ref[h, pl.ds(ii, 1), :]
                    gate = gate + jnp.where(cand >= thr_ref[h, 0:1, :], weight, 0.0)
                rows = slice(half * N_KEYS, (half + 1) * N_KEYS)
                w_scr[pl.ds(e0 + half * N_KEYS, N_KEYS), :] = (gate * _gelu_tanh(act[rows])).astype(BF16)
            return 0

        lax.fori_loop(0, _SUBS_PER_CHUNK, sub, 0)

    @pl.when(c == n_chunks)
    def _():
        for k in range(_SUBS_PER_CHUNK):
            down(k)
        f = acc_scr[...].T
        y = _layer_norm(DN_ALPHA * x1_ref[...] + g2_ref[...] * f)
        y_ref[...] = y * l2w_ref[...] + l2b_ref[...]


def _peer_experts(h2, u, v_t, route, x1, g2, per_row, seq_len, l2w, l2b):
    n = h2.shape[0]
    tg = min(512, seq_len) if not per_row else min(512, n)
    tiles_per_seq = max(seq_len // tg, 1)
    g2_arr, g2_spec = _mod_operand(g2, per_row, tg, tiles_per_seq, 2)
    s1m, c1, s2m, e2, thr = route
    n_chunks = N_EXPERTS // _EXPERT_CHUNK
    rows = lambda i, c: (i, 0)
    full = lambda i, c: (0, 0)
    this_chunk = lambda c: jnp.minimum(c, n_chunks - 1)
    chunk_rows = pl.BlockSpec((PEER_HEADS, _ROWS_PER_CHUNK, tg), lambda i, c: (0, this_chunk(c), i))
    all_rows = pl.BlockSpec((PEER_HEADS, N_KEYS, tg), lambda i, c: (0, 0, i))
    return pl.pallas_call(
        _expert_kernel,
        grid=(n // tg, n_chunks + 1),
        in_specs=[
            pl.BlockSpec((tg, D_MODEL), rows),
            pl.BlockSpec((_EXPERT_CHUNK, D_MODEL), lambda i, c: (this_chunk(c), 0)),
            pl.BlockSpec((_SUBS_PER_CHUNK, D_MODEL, _EXPERT_SUB), lambda i, c: (jnp.maximum(c - 1, 0), 0, 0)),
            chunk_rows, chunk_rows, all_rows, all_rows,
            pl.BlockSpec((PEER_HEADS, 8, tg), lambda i, c: (0, 0, i)),
            pl.BlockSpec((tg, D_MODEL), rows),
            g2_spec,
            pl.BlockSpec((1, D_MODEL), full),
            pl.BlockSpec((1, D_MODEL), full),
        ],
        out_specs=pl.BlockSpec((tg, D_MODEL), rows),
        out_shape=jax.ShapeDtypeStruct((n, D_MODEL), F32),
        scratch_shapes=[
            pltpu.VMEM((D_MODEL, tg), F32),
            pltpu.VMEM((_EXPERT_CHUNK, tg), BF16),
        ],
        compiler_params=_params(("arbitrary", "arbitrary")),
        name="peer_experts_ln2",
    )(h2, u, v_t, s1m, c1, s2m, e2, thr, x1, g2_arr, l2w, l2b)


def _trunk_layer(x, mod, s0, k_past, v_past, wts):
    batch, seq_len, _ = x.shape
    n = batch * seq_len
    per_row = seq_len < 128
    if per_row:
        mod = jnp.repeat(mod, seq_len, axis=0)
    sh1, sc1, g1, sh2, sc2, g2 = [mod[:, k * D_MODEL:(k + 1) * D_MODEL] for k in range(N_MOD)]
    x2d = x.reshape(n, D_MODEL)

    proj, sbk, sbv, glr = _input_projection(x2d, sh1, sc1, per_row, seq_len, wts["w_main"], wts["w_glr"])
    og, s_fin = _gla(proj, glr, wts["w_gate"], wts["b_gate"], wts["norm_w"], s0, batch, seq_len)
    if k_past is None:
        osb = _stick_breaking(proj, sbk, sbv, None, None, batch, seq_len, 0)
    else:
        past_len = k_past.shape[1]
        osb = _stick_breaking(proj, sbk, sbv, k_past.reshape(batch * past_len, D_MODEL),
                              v_past.reshape(batch * past_len, D_MODEL), batch, seq_len, past_len)
    x1, h2 = _merge(og, osb, proj, x2d, g1, sh2, sc2, per_row, seq_len,
                    wts["w_a"], wts["w_b"], wts["w_o"], wts["ln1_w"], wts["ln1_b"])
    route = _peer_route(h2, wts["peer_wq"], wts["peer_keys"])
    y = _peer_experts(h2, wts["peer_u"], wts["peer_vt"], route, x1, g2, per_row, seq_len,
                      wts["ln2_w"], wts["ln2_b"])
    shape_kv = (batch, seq_len, SB_HEADS, SB_HD)
    return y.reshape(batch, seq_len, D_MODEL), s_fin, sbk.reshape(shape_kv), sbv.reshape(shape_kv)


def kernel(x_prompt, x_sample, state_gla, cache_sb_k, cache_sb_v, c_prompt, c_sample, w_ada, b_ada, w_in, w_gla_gate, b_gla_gate, gla_norm_w, w_gla_proj, w_sb_proj, w_o, ln1_w, ln1_b, peer_wq, peer_keys, peer_u, peer_v, ln2_w, ln2_b):
    assert w_in.shape[0] == DEPTH
    l = 0
    glr0 = 2 * GLA_DK + GLA_HEADS * GLA_DV_HEAD
    w = w_in[l]
    wts = {
        "w_main": jnp.concatenate([w[:, :glr0], w[:, glr0 + GLA_GATE_RANK:]], axis=1).astype(BF16),
        "w_glr": jnp.pad(w[:, glr0:glr0 + GLA_GATE_RANK], ((0, 0), (0, LANES - GLA_GATE_RANK))).astype(BF16),
        "w_gate": jnp.pad(w_gla_gate[l], ((0, LANES - GLA_GATE_RANK), (0, 0))).astype(BF16),
        "b_gate": b_gla_gate[l][None, :],
        "norm_w": gla_norm_w[l][None, :],
        "w_a": w_gla_proj[l].astype(BF16),
        "w_b": w_sb_proj[l].astype(BF16),
        "w_o": w_o[l].astype(BF16),
        "ln1_w": ln1_w[l][None, :],
        "ln1_b": ln1_b[l][None, :],
        "peer_wq": peer_wq[l].astype(BF16),
        "peer_keys": peer_keys[l].astype(BF16),
        "peer_u": peer_u[l].astype(BF16),
        "peer_vt": peer_v[l].astype(BF16).reshape(N_EXPERTS // _EXPERT_SUB, _EXPERT_SUB, D_MODEL).transpose(0, 2, 1),
        "ln2_w": ln2_w[l][None, :],
        "ln2_b": ln2_b[l][None, :],
    }
    n_prompt = c_prompt.shape[0]
    mod = _modulation(jnp.concatenate([c_prompt, c_sample], axis=0), w_ada[l], b_ada[l][None, :])

    yp, sp, kp, vp = _trunk_layer(x_prompt, mod[:n_prompt], None, None, None, wts)
    ys, ss, ksm, vsm = _trunk_layer(x_sample, mod[n_prompt:], state_gla[l], cache_sb_k[l], cache_sb_v[l], wts)
    return (yp, ys, sp[None], kp[None], vp[None], ss[None], ksm[None], vsm[None])
```

```python
import functools

import jax
import jax.numpy as jnp
from jax import lax
from jax.experimental import pallas as pl
from jax.experimental.pallas import tpu as pltpu

F32 = jnp.float32
BF16 = jnp.bfloat16

D_MODEL = 1024
N_MOD = 6
GLA_HEADS = 4
GLA_DK_HEAD = 128
GLA_DV_HEAD = 256
GLA_DK = GLA_HEADS * GLA_DK_HEAD
GLA_GATE_RANK = 16
GLA_TAU = 16.0
GLA_BLOCK = 16
SB_HEADS = 16
SB_HD = 64
SB_PAIRS = SB_HEADS // 2
PEER_HEADS = 8
PEER_HALF = 128
N_KEYS = 128
N_EXPERTS = N_KEYS * N_KEYS
PEER_TOPK = 16
DEPTH = 1
DN_ALPHA = (2.0 * DEPTH) ** 0.25
LN_EPS = 1e-5
RMS_EPS = 1e-6

LANES = 128
VMEM_LIMIT = 48 * 1024 * 1024
NEG_INF = float("-inf")

_NT = (((1,), (1,)), ((), ()))


def _dot(a, b):
    return jnp.dot(a, b, preferred_element_type=F32)


def _dot_nt(a, b):
    return lax.dot_general(a, b, _NT, preferred_element_type=F32)


def _layer_norm(x):
    mu = jnp.mean(x, axis=-1, keepdims=True)
    xc = x - mu
    var = jnp.mean(xc * xc, axis=-1, keepdims=True)
    return xc * lax.rsqrt(var + LN_EPS)


def _log_sigmoid(x):
    return jnp.minimum(x, 0.0) - jnp.log1p(jnp.exp(-jnp.abs(x)))


def _split_hi_lo(x):
    hi = x.astype(BF16)
    lo = (x - hi.astype(F32)).astype(BF16)
    return hi, lo


def _params(semantics):
    return pltpu.CompilerParams(dimension_semantics=semantics, vmem_limit_bytes=VMEM_LIMIT)


def _mod_operand(vec, per_row, tile, tiles_per_seq, grid_rank):
    pad = (0,) * (grid_rank - 1)
    if per_row:
        return vec, pl.BlockSpec((tile, D_MODEL), lambda i, *_: (i, 0))
    arr = vec[:, None, :]
    return arr, pl.BlockSpec((None, 1, D_MODEL), lambda i, *_: (i // tiles_per_seq, 0, 0))


def _mod_kernel(c_ref, w_ref, b_ref, o_ref):
    c = c_ref[...]
    s = (c * jax.nn.sigmoid(c)).astype(BF16)
    o_ref[...] = _dot(s, w_ref[...].astype(BF16)) + b_ref[...]


def _modulation(c_all, w_ada, b_ada):
    rows = c_all.shape[0]
    return pl.pallas_call(
        _mod_kernel,
        grid=(N_MOD,),
        in_specs=[
            pl.BlockSpec((rows, D_MODEL), lambda j: (0, 0)),
            pl.BlockSpec((D_MODEL, D_MODEL), lambda j: (0, j)),
            pl.BlockSpec((1, D_MODEL), lambda j: (0, j)),
        ],
        out_specs=pl.BlockSpec((rows, D_MODEL), lambda j: (0, j)),
        out_shape=jax.ShapeDtypeStruct((rows, N_MOD * D_MODEL), F32),
        compiler_params=_params(("arbitrary",)),
        name="adaln_mod",
    )(c_all, w_ada, b_ada)


_N_COL_TILES = 8
_COL_SK = 4
_COL_SV = 5


def _inproj_kernel(x_ref, sh_ref, sc_ref, w_ref, wglr_ref, proj_ref, sbk_ref, sbv_ref, glr_ref, h_scr):
    j = pl.program_id(1)

    @pl.when(j == 0)
    def _():
        h = _layer_norm(x_ref[...]) * (1.0 + sc_ref[...]) + sh_ref[...]
        hb = h.astype(BF16)
        h_scr[...] = hb
        glr_ref[...] = _dot(hb, wglr_ref[...])

    acc = _dot(h_scr[...], w_ref[...])

    @pl.when(j == _COL_SK)
    def _():
        sbk_ref[...] = acc

    @pl.when(j == _COL_SV)
    def _():
        sbv_ref[...] = acc

    @pl.when((j < _COL_SK) | (j > _COL_SV))
    def _():
        proj_ref[...] = acc


def _proj_col(j):
    return jnp.where(j < _COL_SK, j, jnp.where(j <= _COL_SV, _COL_SK - 1, j - 2))


def _input_projection(x2d, shift, scale, per_row, seq_len, w_main, w_glr):
    n = x2d.shape[0]
    tm = min(1024, seq_len) if not per_row else min(512, n)
    tiles_per_seq = max(seq_len // tm, 1)
    sh_arr, sh_spec = _mod_operand(shift, per_row, tm, tiles_per_seq, 2)
    sc_arr, sc_spec = _mod_operand(scale, per_row, tm, tiles_per_seq, 2)
    row_out = lambda i, j: (i, 0)
    return pl.pallas_call(
        _inproj_kernel,
        grid=(n // tm, _N_COL_TILES),
        in_specs=[
            pl.BlockSpec((tm, D_MODEL), row_out),
            sh_spec,
            sc_spec,
            pl.BlockSpec((D_MODEL, D_MODEL), lambda i, j: (0, j)),
            pl.BlockSpec((D_MODEL, LANES), lambda i, j: (0, 0)),
        ],
        out_specs=[
            pl.BlockSpec((tm, D_MODEL), lambda i, j: (i, _proj_col(j))),
            pl.BlockSpec((tm, D_MODEL), row_out),
            pl.BlockSpec((tm, D_MODEL), row_out),
            pl.BlockSpec((tm, LANES), row_out),
        ],
        out_shape=[
            jax.ShapeDtypeStruct((n, 6 * D_MODEL), F32),
            jax.ShapeDtypeStruct((n, D_MODEL), F32),
            jax.ShapeDtypeStruct((n, D_MODEL), F32),
            jax.ShapeDtypeStruct((n, LANES), F32),
        ],
        scratch_shapes=[pltpu.VMEM((tm, D_MODEL), BF16)],
        compiler_params=_params(("arbitrary", "arbitrary")),
        name="ln_inproj",
    )(x2d, sh_arr, sc_arr, w_main, w_glr)


_GLA_CHUNK = 128


def _pad_rows(x, rows):
    if x.shape[0] == rows:
        return x
    return jnp.concatenate([x, jnp.zeros((rows - x.shape[0], x.shape[1]), x.dtype)], axis=0)


def _gla_kernel(*refs, t_real, has_s0):
    if has_s0:
        (qk_ref, v_ref, gog_ref, glr_ref, wg_ref, bg_ref, nw_ref, s0_ref, og_ref, sfin_ref, st_scr) = refs
    else:
        (qk_ref, v_ref, gog_ref, glr_ref, wg_ref, bg_ref, nw_ref, og_ref, sfin_ref, st_scr) = refs
        s0_ref = None
    t = pl.program_id(1)
    tc = _GLA_CHUNK
    n_blocks = t_real // GLA_BLOCK

    @pl.when(t == 0)
    def _():
        for hh in range(GLA_HEADS):
            if has_s0:
                st_scr[hh] = s0_ref[hh].T
            else:
                st_scr[hh] = jnp.zeros((GLA_DV_HEAD, GLA_DK_HEAD), F32)

    row = lax.broadcasted_iota(jnp.int32, (tc, tc), 0)
    col = lax.broadcasted_iota(jnp.int32, (tc, tc), 1)
    same_block = (row // GLA_BLOCK) == (col // GLA_BLOCK)
    causal = same_block & (col <= row)
    m_cum = causal.astype(BF16)
    m_tot = same_block.astype(BF16)
    row_block = lax.broadcasted_iota(jnp.int32, (tc, GLA_DK_HEAD), 0) // GLA_BLOCK

    glr = _pad_rows(glr_ref[...], tc).astype(BF16)
    for hh in range(GLA_HEADS):
        ks = slice(hh * GLA_DK_HEAD, (hh + 1) * GLA_DK_HEAD)
        vs = slice(hh * GLA_DV_HEAD, (hh + 1) * GLA_DV_HEAD)
        q = _pad_rows(qk_ref[:, ks], tc) * (GLA_DK_HEAD ** -0.5)
        k = _pad_rows(qk_ref[:, GLA_DK + hh * GLA_DK_HEAD:GLA_DK + (hh + 1) * GLA_DK_HEAD], tc)
        v = _pad_rows(v_ref[:, vs], tc)
        vb = v.astype(BF16)
        v_t = v.T.astype(BF16)

        pre = _dot(glr, wg_ref[:, ks]) + bg_ref[:, ks]
        log_a = _log_sigmoid(pre) / GLA_TAU
        if t_real < tc:
            log_a = jnp.where(lax.broadcasted_iota(jnp.int32, log_a.shape, 0) < t_real, log_a, 0.0)
        la_hi, la_lo = _split_hi_lo(log_a)
        b = _dot(m_cum, la_hi) + _dot(m_cum, la_lo)
        b_tot = _dot(m_tot, la_hi) + _dot(m_tot, la_lo)
        qe = (q * jnp.exp(b)).astype(BF16)
        ke = (k * jnp.exp(-b)).astype(BF16)
        kd = (k * jnp.exp(b_tot - b)).astype(BF16)
        decay = jnp.exp(b_tot)

        att = jnp.where(causal, _dot_nt(qe, ke), 0.0).astype(BF16)
        o_intra = _dot(att, vb)

        st = st_scr[hh]
        outs = []
        for r in range(n_blocks):
            rows = slice(r * GLA_BLOCK, (r + 1) * GLA_BLOCK)
            outs.append(o_intra[rows] + _dot_nt(qe[rows], st.astype(BF16)))
            kd_r = jnp.where(row_block == r, kd, jnp.zeros_like(kd))
            st = decay[r * GLA_BLOCK:r * GLA_BLOCK + 1, :] * st + _dot(v_t, kd_r)
        st_scr[hh] = st

        o = outs[0] if n_blocks == 1 else jnp.concatenate(outs, axis=0)
        o = o * lax.rsqrt(jnp.mean(o * o, axis=-1, keepdims=True) + RMS_EPS) * nw_ref[...]
        g = gog_ref[:, vs]
        og_ref[:, vs] = (o * (g * jax.nn.sigmoid(g))).astype(BF16)

    @pl.when(t == pl.num_programs(1) - 1)
    def _():
        for hh in range(GLA_HEADS):
            sfin_ref[hh] = st_scr[hh].T


def _gla(proj, glr, w_gate, b_gate, norm_w, s0, batch, seq_len):
    n = batch * seq_len
    tc = min(_GLA_CHUNK, seq_len)
    assert seq_len % tc == 0 and tc % GLA_BLOCK == 0
    steps = seq_len // tc
    rows = lambda b, t: (b * steps + t, 0)
    in_specs = [
        pl.BlockSpec((tc, D_MODEL), lambda b, t: (b * steps + t, 0)),
        pl.BlockSpec((tc, D_MODEL), lambda b, t: (b * steps + t, 1)),
        pl.BlockSpec((tc, D_MODEL), lambda b, t: (b * steps + t, 2)),
        pl.BlockSpec((tc, LANES), rows),
        pl.BlockSpec((LANES, GLA_DK), lambda b, t: (0, 0)),
        pl.BlockSpec((1, GLA_DK), lambda b, t: (0, 0)),
        pl.BlockSpec((1, GLA_DV_HEAD), lambda b, t: (0, 0)),
    ]
    operands = [proj, proj, proj, glr, w_gate, b_gate, norm_w]
    state_spec = pl.BlockSpec((None, GLA_HEADS, GLA_DK_HEAD, GLA_DV_HEAD), lambda b, t: (b, 0, 0, 0))
    if s0 is not None:
        in_specs.append(state_spec)
        operands.append(s0)
    return pl.pallas_call(
        functools.partial(_gla_kernel, t_real=tc, has_s0=s0 is not None),
        grid=(batch, steps),
        in_specs=in_specs,
        out_specs=[pl.BlockSpec((tc, D_MODEL), rows), state_spec],
        out_shape=[
            jax.ShapeDtypeStruct((n, D_MODEL), BF16),
            jax.ShapeDtypeStruct((batch, GLA_HEADS, GLA_DK_HEAD, GLA_DV_HEAD), F32),
        ],
        scratch_shapes=[pltpu.VMEM((GLA_HEADS, GLA_DV_HEAD, GLA_DK_HEAD), F32)],
        compiler_params=_params(("arbitrary", "arbitrary")),
        name="gla_recurrence",
    )(*operands)


_SB_TILE = 128
_SB_LANES = 256
_SB_DEAD = -88.0


def _sb_kernel(q_ref, kd_ref, vd_ref, kp_ref, vp_ref, o_ref, *, t_real, past_blocks):
    tq = _SB_TILE
    heads = _SB_LANES // SB_HD
    n_past = pl.program_id(2) if past_blocks is None else past_blocks

    lane = lax.broadcasted_iota(jnp.int32, (tq, _SB_LANES), 1)
    q = _pad_rows(q_ref[...], tq) * (SB_HD ** -0.5)
    in_head = [(lane >= hd * SB_HD) & (lane < (hd + 1) * SB_HD) for hd in range(heads)]
    q_heads = [jnp.where(m, q, 0.0).astype(BF16) for m in in_head]

    r2 = lax.broadcasted_iota(jnp.int32, (2 * tq, 2 * tq), 0)
    c2 = lax.broadcasted_iota(jnp.int32, (2 * tq, 2 * tq), 1)
    suffix = jnp.concatenate([(r2 > c2).astype(BF16), jnp.ones((2 * tq, LANES), BF16)], axis=1)
    row = lax.broadcasted_iota(jnp.int32, (tq, 2 * tq), 0)
    col = lax.broadcasted_iota(jnp.int32, (tq, 2 * tq), 1)

    def one_head(qh, kb, vb, carry, mask):
        z = _dot_nt(qh, kb)
        soft = jnp.log1p(jnp.exp(-jnp.abs(z)))
        log_rest = jnp.where(mask, -jnp.maximum(z, 0.0) - soft, 0.0)
        log_beta = jnp.minimum(z, 0.0) - soft
        hi, lo = _split_hi_lo(log_rest)
        sums = _dot(hi, suffix) + _dot(lo, suffix)
        after = sums[:, :2 * tq]
        total = sums[:, 2 * tq:]
        if carry is not None:
            after = after + jnp.concatenate([carry, carry], axis=1)
            total = total + carry
        a = jnp.where(mask, jnp.exp(log_beta + after), 0.0)
        return _dot(a.astype(BF16), vb), total

    def step(first_rows, second_k, second_v, carries, mask):
        kb = jnp.concatenate([kp_ref[pl.ds(first_rows, tq), :], second_k], axis=0).astype(BF16)
        vb = jnp.concatenate([vp_ref[pl.ds(first_rows, tq), :], second_v], axis=0).astype(BF16)
        acc = None
        new_carries = []
        for hd in range(heads):
            pv, c = one_head(q_heads[hd], kb, vb, None if carries is None else carries[hd], mask)
            new_carries.append(c)
            acc = jnp.where(in_head[hd], pv, 0.0 if acc is None else acc)
        return acc, new_carries

    def live(carries):
        top = carries[0]
        for c in carries[1:]:
            top = jnp.maximum(top, c)
        return (jnp.max(top) > _SB_DEAD).astype(jnp.int32)

    prev_cols = jnp.where(n_past > 0, tq, 0)
    visible = (col < prev_cols) | ((col >= tq) & (col - tq < row))
    prev_start = pl.multiple_of(jnp.maximum(n_past - 1, 0) * tq, tq)
    acc, carries = step(prev_start, _pad_rows(kd_ref[...], tq), _pad_rows(vd_ref[...], tq), None, visible)

    def more(state):
        return (state[0] >= 0) & (state[1] > 0)

    def older(state):
        j, _, acc = state[0], state[1], state[2]
        carries = list(state[3:])
        first = pl.multiple_of(jnp.maximum(j - 1, 0) * tq, tq)
        second = pl.multiple_of(j * tq, tq)
        mask = (col >= jnp.where(j > 0, 0, tq))
        pv, carries = step(first, kp_ref[pl.ds(second, tq), :], vp_ref[pl.ds(second, tq), :], carries, mask)
        return (j - 2, live(carries), acc + pv, *carries)

    state = lax.while_loop(more, older, (n_past - 2, live(carries), acc, *carries))
    o_ref[...] = state[2][:t_real].astype(BF16)


def _stick_breaking(proj, sbk, sbv, k_past, v_past, batch, seq_len, past_len):
    n = batch * seq_len
    tq = min(_SB_TILE, seq_len)
    assert seq_len % tq == 0
    q_tiles = seq_len // tq
    groups = D_MODEL // _SB_LANES
    sq_col0 = 3 * groups
    own = k_past is None
    if own:
        k_past, v_past, past_len = sbk, sbv, seq_len
    assert past_len % _SB_TILE == 0
    tile = lambda b, p, i: (b * q_tiles + i, p)
    whole = lambda b, p, i: (b, p)
    return pl.pallas_call(
        functools.partial(_sb_kernel, t_real=tq, past_blocks=None if own else past_len // _SB_TILE),
        grid=(batch, groups, q_tiles),
        in_specs=[
            pl.BlockSpec((tq, _SB_LANES), lambda b, p, i: (b * q_tiles + i, sq_col0 + p)),
            pl.BlockSpec((tq, _SB_LANES), tile),
            pl.BlockSpec((tq, _SB_LANES), tile),
            pl.BlockSpec((past_len, _SB_LANES), whole),
            pl.BlockSpec((past_len, _SB_LANES), whole),
        ],
        out_specs=pl.BlockSpec((tq, _SB_LANES), tile),
        out_shape=jax.ShapeDtypeStruct((n, D_MODEL), BF16),
        compiler_params=_params(("arbitrary", "arbitrary", "arbitrary")),
        name="stick_breaking",
    )(proj, sbk, sbv, k_past, v_past)


def _merge_kernel(og_ref, osb_ref, mga_ref, mgb_ref, x_ref, g1_ref, sh2_ref, sc2_ref,
                  wa_ref, wb_ref, wo_ref, l1w_ref, l1b_ref, x1_ref, h2_ref):
    br_a = _dot(og_ref[...], wa_ref[...])
    br_b = _dot(osb_ref[...], wb_ref[...])
    mixed = jax.nn.sigmoid(mga_ref[...]) * br_a + jax.nn.sigmoid(mgb_ref[...]) * br_b
    mix = _dot(mixed.astype(BF16), wo_ref[...])
    x1 = _layer_norm(DN_ALPHA * x_ref[...] + g1_ref[...] * mix) * l1w_ref[...] + l1b_ref[...]
    x1_ref[...] = x1
    h2_ref[...] = (_layer_norm(x1) * (1.0 + sc2_ref[...]) + sh2_ref[...]).astype(BF16)


def _merge(og, osb, proj, x2d, g1, sh2, sc2, per_row, seq_len, wa, wb, wo, l1w, l1b):
    n = x2d.shape[0]
    tm = min(512, seq_len) if not per_row else min(512, n)
    tiles_per_seq = max(seq_len // tm, 1)
    mods = [_mod_operand(v, per_row, tm, tiles_per_seq, 1) for v in (g1, sh2, sc2)]
    rows = lambda i: (i, 0)
    full = lambda i: (0, 0)
    wspec = pl.BlockSpec((D_MODEL, D_MODEL), full)
    vspec = pl.BlockSpec((1, D_MODEL), full)
    return pl.pallas_call(
        _merge_kernel,
        grid=(n // tm,),
        in_specs=[
            pl.BlockSpec((tm, D_MODEL), rows),
            pl.BlockSpec((tm, D_MODEL), rows),
            pl.BlockSpec((tm, D_MODEL), lambda i: (i, 4)),
            pl.BlockSpec((tm, D_MODEL), lambda i: (i, 5)),
            pl.BlockSpec((tm, D_MODEL), rows),
            mods[0][1], mods[1][1], mods[2][1],
            wspec, wspec, wspec, vspec, vspec,
        ],
        out_specs=[pl.BlockSpec((tm, D_MODEL), rows), pl.BlockSpec((tm, D_MODEL), rows)],
        out_shape=[jax.ShapeDtypeStruct((n, D_MODEL), F32), jax.ShapeDtypeStruct((n, D_MODEL), BF16)],
        compiler_params=_params(("arbitrary",)),
        name="merge_ln1",
    )(og, osb, proj, proj, x2d, mods[0][0], mods[1][0], mods[2][0], wa, wb, wo, l1w, l1b)


def _extract_top(s, count):
    m = s.shape[0]
    idx = lax.broadcasted_iota(jnp.int32, s.shape, 0).astype(F32)
    rank = lax.broadcasted_iota(jnp.int32, (count, s.shape[1]), 0)

    def step(r, carry):
        cur, taken, vals = carry
        top = jnp.max(cur, axis=0, keepdims=True)
        first = jnp.min(jnp.where(cur == top, idx, float(m)), axis=0, keepdims=True)
        hit = idx == first
        vals = jnp.where(rank == r, top, vals)
        return jnp.where(hit, NEG_INF, cur), jnp.where(hit, 1.0, taken), vals

    init = (s, jnp.zeros(s.shape, F32), jnp.zeros((count, s.shape[1]), F32))
    _, taken, vals = lax.fori_loop(0, count, step, init)
    return vals, taken > 0.5


def _pair_candidates(v1, v2):
    k = PEER_TOPK
    row8 = lax.broadcasted_iota(jnp.int32, (8, v1.shape[1]), 0)
    groups = [v1[0:1, :] + v2]
    for a in range(1, 8):
        groups.append(jnp.where(row8 < k // (a + 1), v1[a:a + 1, :] + v2[0:8, :], NEG_INF))
    groups.append(v1[8:k, :] + v2[0:1, :])
    return jnp.concatenate(groups, axis=0)


def _route_kernel(h2_ref, wq_ref, keys_ref, s1m_ref, c1_ref, s2m_ref, e2_ref, thr_ref):
    q = _dot(h2_ref[...], wq_ref[...]).astype(BF16)
    s1 = _dot_nt(keys_ref[0], q[:, :PEER_HALF])
    s2 = _dot_nt(keys_ref[1], q[:, PEER_HALF:])
    v1, in1 = _extract_top(s1, PEER_TOPK)
    v2, in2 = _extract_top(s2, PEER_TOPK)
    top, _ = _extract_top(_pair_candidates(v1, v2), PEER_TOPK)
    best = top[0:1, :]
    z = jnp.sum(jnp.exp(top - best), axis=0, keepdims=True)
    s1m_ref[...] = jnp.where(in1, s1, NEG_INF)
    s2m_ref[...] = jnp.where(in2, s2, NEG_INF)
    c1_ref[...] = jnp.where(in1, jnp.exp(s1 - v1[0:1, :]), 0.0) / z
    e2_ref[...] = jnp.where(in2, jnp.exp(s2 - v2[0:1, :]), 0.0)
    thr_ref[...] = jnp.broadcast_to(top[PEER_TOPK - 1:PEER_TOPK, :], thr_ref.shape)


def _peer_route(h2, wq, keys):
    n = h2.shape[0]
    tn = min(256, n)
    per_head = pl.BlockSpec((None, N_KEYS, tn), lambda i, h: (h, 0, i))
    shape = jax.ShapeDtypeStruct((PEER_HEADS, N_KEYS, n), F32)
    return pl.pallas_call(
        _route_kernel,
        grid=(n // tn, PEER_HEADS),
        in_specs=[
            pl.BlockSpec((tn, D_MODEL), lambda i, h: (i, 0)),
            pl.BlockSpec((D_MODEL, 2 * PEER_HALF), lambda i, h: (0, h)),
            pl.BlockSpec((None, 2, N_KEYS, PEER_HALF), lambda i, h: (h, 0, 0, 0)),
        ],
        out_specs=[per_head, per_head, per_head, per_head,
                   pl.BlockSpec((None, 8, tn), lambda i, h: (h, 0, i))],
        out_shape=[shape, shape, shape, shape, jax.ShapeDtypeStruct((PEER_HEADS, 8, n), F32)],
        compiler_params=_params(("arbitrary", "arbitrary")),
        name="peer_route",
    )(h2, wq, keys)


_EXPERT_CHUNK = 1024
_ROWS_PER_CHUNK = _EXPERT_CHUNK // N_KEYS
_EXPERT_SUB = 256
_SUBS_PER_CHUNK = _EXPERT_CHUNK // _EXPERT_SUB


def _gelu_tanh(x):
    c = 2.0 * 0.7978845608028654
    return x / (1.0 + jnp.exp(x * (-c - (c * 0.044715) * (x * x))))


def _expert_kernel(h2_ref, u_ref, vt_ref, s1m_ref, c1_ref, s2m_ref, e2_ref, thr_ref,
                   x1_ref, g2_ref, l2w_ref, l2b_ref, y_ref, acc_scr, w_scr):
    c = pl.program_id(1)
    n_chunks = pl.num_programs(1) - 1
    tokens = w_scr.shape[1]
    keys_per_sub = _EXPERT_SUB // N_KEYS

    @pl.when(c == 0)
    def _():
        acc_scr[...] = jnp.zeros_like(acc_scr)
        w_scr[...] = jnp.zeros_like(w_scr)

    def down(k):
        e0 = pl.multiple_of(k * _EXPERT_SUB, _EXPERT_SUB)
        acc_scr[...] += _dot(vt_ref[k], w_scr[pl.ds(e0, _EXPERT_SUB), :])

    @pl.when(c < n_chunks)
    def _():
        def sub(k, _):
            down(k)
            e0 = pl.multiple_of(k * _EXPERT_SUB, _EXPERT_SUB)
            act = _dot_nt(u_ref[pl.ds(e0, _EXPERT_SUB), :], h2_ref[...])
            for half in range(keys_per_sub):
                ii = k * keys_per_sub + half
                gate = jnp.zeros((N_KEYS, tokens), F32)
                for h in range(PEER_HEADS):
                    cand = s1m_ref[h, pl.ds(ii, 1), :] + s2m_ref[h]
                    weight = e2_ref[h] * c1_ref[h, pl.ds(ii, 1), :]
                    gate = gate + jnp.where(cand >= thr_ref[h, 0:1, :], weight, 0.0)
                rows = slice(half * N_KEYS, (half + 1) * N_KEYS)
                w_scr[pl.ds(e0 + half * N_KEYS, N_KEYS), :] = (gate * _gelu_tanh(act[rows])).astype(BF16)
            return 0

        lax.fori_loop(0, _SUBS_PER_CHUNK, sub, 0)

    @pl.when(c == n_chunks)
    def _():
        for k in range(_SUBS_PER_CHUNK):
            down(k)
        f = acc_scr[...].T
        y = _layer_norm(DN_ALPHA * x1_ref[...] + g2_ref[...] * f)
        y_ref[...] = y * l2w_ref[...] + l2b_ref[...]


def _peer_experts(h2, u, v_t, route, x1, g2, per_row, seq_len, l2w, l2b):
    n = h2.shape[0]
    tg = min(512, seq_len) if not per_row else min(512, n)
    tiles_per_seq = max(seq_len // tg, 1)
    g2_arr, g2_spec = _mod_operand(g2, per_row, tg, tiles_per_seq, 2)
    s1m, c1, s2m, e2, thr = route
    n_chunks = N_EXPERTS // _EXPERT_CHUNK
    rows = lambda i, c: (i, 0)
    full = lambda i, c: (0, 0)
    this_chunk = lambda c: jnp.minimum(c, n_chunks - 1)
    chunk_rows = pl.BlockSpec((PEER_HEADS, _ROWS_PER_CHUNK, tg), lambda i, c: (0, this_chunk(c), i))
    all_rows = pl.BlockSpec((PEER_HEADS, N_KEYS, tg), lambda i, c: (0, 0, i))
    return pl.pallas_call(
        _expert_kernel,
        grid=(n // tg, n_chunks + 1),
        in_specs=[
            pl.BlockSpec((tg, D_MODEL), rows),
            pl.BlockSpec((_EXPERT_CHUNK, D_MODEL), lambda i, c: (this_chunk(c), 0)),
            pl.BlockSpec((_SUBS_PER_CHUNK, D_MODEL, _EXPERT_SUB), lambda i, c: (jnp.maximum(c - 1, 0), 0, 0)),
            chunk_rows, chunk_rows, all_rows, all_rows,
            pl.BlockSpec((PEER_HEADS, 8, tg), lambda i, c: (0, 0, i)),
            pl.BlockSpec((tg, D_MODEL), rows),
            g2_spec,
            pl.BlockSpec((1, D_MODEL), full),
            pl.BlockSpec((1, D_MODEL), full),
        ],
        out_specs=pl.BlockSpec((tg, D_MODEL), rows),
        out_shape=jax.ShapeDtypeStruct((n, D_MODEL), F32),
        scratch_shapes=[
            pltpu.VMEM((D_MODEL, tg), F32),
            pltpu.VMEM((_EXPERT_CHUNK, tg), BF16),
        ],
        compiler_params=_params(("arbitrary", "arbitrary")),
        name="peer_experts_ln2",
    )(h2, u, v_t, s1m, c1, s2m, e2, thr, x1, g2_arr, l2w, l2b)


def _trunk_layer(x, mod, s0, k_past, v_past, wts):
    batch, seq_len, _ = x.shape
    n = batch * seq_len
    per_row = seq_len < 128
    if per_row:
        mod = jnp.repeat(mod, seq_len, axis=0)
    sh1, sc1, g1, sh2, sc2, g2 = [mod[:, k * D_MODEL:(k + 1) * D_MODEL] for k in range(N_MOD)]
    x2d = x.reshape(n, D_MODEL)

    proj, sbk, sbv, glr = _input_projection(x2d, sh1, sc1, per_row, seq_len, wts["w_main"], wts["w_glr"])
    og, s_fin = _gla(proj, glr, wts["w_gate"], wts["b_gate"], wts["norm_w"], s0, batch, seq_len)
    if k_past is None:
        osb = _stick_breaking(proj, sbk, sbv, None, None, batch, seq_len, 0)
    else:
        past_len = k_past.shape[1]
        osb = _stick_breaking(proj, sbk, sbv, k_past.reshape(batch * past_len, D_MODEL),
                              v_past.reshape(batch * past_len, D_MODEL), batch, seq_len, past_len)
    x1, h2 = _merge(og, osb, proj, x2d, g1, sh2, sc2, per_row, seq_len,
                    wts["w_a"], wts["w_b"], wts["w_o"], wts["ln1_w"], wts["ln1_b"])
    route = _peer_route(h2, wts["peer_wq"], wts["peer_keys"])
    y = _peer_experts(h2, wts["peer_u"], wts["peer_vt"], route, x1, g2, per_row, seq_len,
                      wts["ln2_w"], wts["ln2_b"])
    shape_kv = (batch, seq_len, SB_HEADS, SB_HD)
    return y.reshape(batch, seq_len, D_MODEL), s_fin, sbk.reshape(shape_kv), sbv.reshape(shape_kv)


def kernel(x_prompt, x_sample, state_gla, cache_sb_k, cache_sb_v, c_prompt, c_sample, w_ada, b_ada, w_in, w_gla_gate, b_gla_gate, gla_norm_w, w_gla_proj, w_sb_proj, w_o, ln1_w, ln1_b, peer_wq, peer_keys, peer_u, peer_v, ln2_w, ln2_b):
    assert w_in.shape[0] == DEPTH
    l = 0
    glr0 = 2 * GLA_DK + GLA_HEADS * GLA_DV_HEAD
    w = w_in[l]
    wts = {
        "w_main": jnp.concatenate([w[:, :glr0], w[:, glr0 + GLA_GATE_RANK:]], axis=1).astype(BF16),
        "w_glr": jnp.pad(w[:, glr0:glr0 + GLA_GATE_RANK], ((0, 0), (0, LANES - GLA_GATE_RANK))).astype(BF16),
        "w_gate": jnp.pad(w_gla_gate[l], ((0, LANES - GLA_GATE_RANK), (0, 0))).astype(BF16),
        "b_gate": b_gla_gate[l][None, :],
        "norm_w": gla_norm_w[l][None, :],
        "w_a": w_gla_proj[l].astype(BF16),
        "w_b": w_sb_proj[l].astype(BF16),
        "w_o": w_o[l].astype(BF16),
        "ln1_w": ln1_w[l][None, :],
        "ln1_b": ln1_b[l][None, :],
        "peer_wq": peer_wq[l].astype(BF16),
        "peer_keys": peer_keys[l].astype(BF16),
        "peer_u": peer_u[l].astype(BF16),
        "peer_vt": peer_v[l].astype(BF16).reshape(N_EXPERTS // _EXPERT_SUB, _EXPERT_SUB, D_MODEL).transpose(0, 2, 1),
        "ln2_w": ln2_w[l][None, :],
        "ln2_b": ln2_b[l][None, :],
    }
    n_prompt = c_prompt.shape[0]
    mod = _modulation(jnp.concatenate([c_prompt, c_sample], axis=0), w_ada[l], b_ada[l][None, :])

    yp, sp, kp, vp = _trunk_layer(x_prompt, mod[:n_prompt], None, None, None, wts)
    ys, ss, ksm, vsm = _trunk_layer(x_sample, mod[n_prompt:], state_gla[l], cache_sb_k[l], cache_sb_v[l], wts)
    return (yp, ys, sp[None], kp[None], vp[None], ss[None], ksm[None], vsm[None])
```

```python
import functools

import jax
import jax.numpy as jnp
from jax import lax
from jax.experimental import pallas as pl
from jax.experimental.pallas import tpu as pltpu

F32 = jnp.float32
BF16 = jnp.bfloat16

D_MODEL = 1024
N_MOD = 6
GLA_HEADS = 4
GLA_DK_HEAD = 128
GLA_DV_HEAD = 256
GLA_DK = GLA_HEADS * GLA_DK_HEAD
GLA_GATE_RANK = 16
GLA_TAU = 16.0
GLA_BLOCK = 16
SB_HEADS = 16
SB_HD = 64
SB_PAIRS = SB_HEADS // 2
PEER_HEADS = 8
PEER_HALF = 128
N_KEYS = 128
N_EXPERTS = N_KEYS * N_KEYS
PEER_TOPK = 16
DEPTH = 1
DN_ALPHA = (2.0 * DEPTH) ** 0.25
LN_EPS = 1e-5
RMS_EPS = 1e-6

LANES = 128
VMEM_LIMIT = 48 * 1024 * 1024
NEG_INF = float("-inf")

_NT = (((1,), (1,)), ((), ()))


def _dot(a, b):
    return jnp.dot(a, b, preferred_element_type=F32)


def _dot_nt(a, b):
    return lax.dot_general(a, b, _NT, preferred_element_type=F32)


def _layer_norm(x):
    mu = jnp.mean(x, axis=-1, keepdims=True)
    xc = x - mu
    var = jnp.mean(xc * xc, axis=-1, keepdims=True)
    return xc * lax.rsqrt(var + LN_EPS)


def _log_sigmoid(x):
    return jnp.minimum(x, 0.0) - jnp.log1p(jnp.exp(-jnp.abs(x)))


def _split_hi_lo(x):
    hi = x.astype(BF16)
    lo = (x - hi.astype(F32)).astype(BF16)
    return hi, lo


def _params(semantics):
    return pltpu.CompilerParams(dimension_semantics=semantics, vmem_limit_bytes=VMEM_LIMIT)


def _mod_operand(vec, per_row, tile, tiles_per_seq, grid_rank):
    pad = (0,) * (grid_rank - 1)
    if per_row:
        return vec, pl.BlockSpec((tile, D_MODEL), lambda i, *_: (i, 0))
    arr = vec[:, None, :]
    return arr, pl.BlockSpec((None, 1, D_MODEL), lambda i, *_: (i // tiles_per_seq, 0, 0))


def _mod_kernel(c_ref, w_ref, b_ref, o_ref):
    c = c_ref[...]
    s = (c * jax.nn.sigmoid(c)).astype(BF16)
    o_ref[...] = _dot(s, w_ref[...].astype(BF16)) + b_ref[...]


def _modulation(c_all, w_ada, b_ada):
    rows = c_all.shape[0]
    return pl.pallas_call(
        _mod_kernel,
        grid=(N_MOD,),
        in_specs=[
            pl.BlockSpec((rows, D_MODEL), lambda j: (0, 0)),
            pl.BlockSpec((D_MODEL, D_MODEL), lambda j: (0, j)),
            pl.BlockSpec((1, D_MODEL), lambda j: (0, j)),
        ],
        out_specs=pl.BlockSpec((rows, D_MODEL), lambda j: (0, j)),
        out_shape=jax.ShapeDtypeStruct((rows, N_MOD * D_MODEL), F32),
        compiler_params=_params(("arbitrary",)),
        name="adaln_mod",
    )(c_all, w_ada, b_ada)


_N_COL_TILES = 8
_COL_SK = 4
_COL_SV = 5


def _inproj_kernel(x_ref, sh_ref, sc_ref, w_ref, wglr_ref, proj_ref, sbk_ref, sbv_ref, glr_ref, h_scr):
    j = pl.program_id(1)

    @pl.when(j == 0)
    def _():
        h = _layer_norm(x_ref[...]) * (1.0 + sc_ref[...]) + sh_ref[...]
        hb = h.astype(BF16)
        h_scr[...] = hb
        glr_ref[...] = _dot(hb, wglr_ref[...])

    acc = _dot(h_scr[...], w_ref[...])

    @pl.when(j == _COL_SK)
    def _():
        sbk_ref[...] = acc

    @pl.when(j == _COL_SV)
    def _():
        sbv_ref[...] = acc

    @pl.when((j < _COL_SK) | (j > _COL_SV))
    def _():
        proj_ref[...] = acc


def _proj_col(j):
    return jnp.where(j < _COL_SK, j, jnp.where(j <= _COL_SV, _COL_SK - 1, j - 2))


def _input_projection(x2d, shift, scale, per_row, seq_len, w_main, w_glr):
    n = x2d.shape[0]
    tm = min(1024, seq_len) if not per_row else min(512, n)
    tiles_per_seq = max(seq_len // tm, 1)
    sh_arr, sh_spec = _mod_operand(shift, per_row, tm, tiles_per_seq, 2)
    sc_arr, sc_spec = _mod_operand(scale, per_row, tm, tiles_per_seq, 2)
    row_out = lambda i, j: (i, 0)
    return pl.pallas_call(
        _inproj_kernel,
        grid=(n // tm, _N_COL_TILES),
        in_specs=[
            pl.BlockSpec((tm, D_MODEL), row_out),
            sh_spec,
            sc_spec,
            pl.BlockSpec((D_MODEL, D_MODEL), lambda i, j: (0, j)),
            pl.BlockSpec((D_MODEL, LANES), lambda i, j: (0, 0)),
        ],
        out_specs=[
            pl.BlockSpec((tm, D_MODEL), lambda i, j: (i, _proj_col(j))),
            pl.BlockSpec((tm, D_MODEL), row_out),
            pl.BlockSpec((tm, D_MODEL), row_out),
            pl.BlockSpec((tm, LANES), row_out),
        ],
        out_shape=[
            jax.ShapeDtypeStruct((n, 6 * D_MODEL), F32),
            jax.ShapeDtypeStruct((n, D_MODEL), F32),
            jax.ShapeDtypeStruct((n, D_MODEL), F32),
            jax.ShapeDtypeStruct((n, LANES), F32),
        ],
        scratch_shapes=[pltpu.VMEM((tm, D_MODEL), BF16)],
        compiler_params=_params(("arbitrary", "arbitrary")),
        name="ln_inproj",
    )(x2d, sh_arr, sc_arr, w_main, w_glr)


_GLA_CHUNK = 128


def _pad_rows(x, rows):
    if x.shape[0] == rows:
        return x
    return jnp.concatenate([x, jnp.zeros((rows - x.shape[0], x.shape[1]), x.dtype)], axis=0)


def _gla_kernel(*refs, t_real, has_s0):
    if has_s0:
        (qk_ref, v_ref, gog_ref, glr_ref, wg_ref, bg_ref, nw_ref, s0_ref, og_ref, sfin_ref, st_scr) = refs
    else:
        (qk_ref, v_ref, gog_ref, glr_ref, wg_ref, bg_ref, nw_ref, og_ref, sfin_ref, st_scr) = refs
        s0_ref = None
    t = pl.program_id(1)
    tc = _GLA_CHUNK
    n_blocks = t_real // GLA_BLOCK

    @pl.when(t == 0)
    def _():
        for hh in range(GLA_HEADS):
            if has_s0:
                st_scr[hh] = s0_ref[hh].T
            else:
                st_scr[hh] = jnp.zeros((GLA_DV_HEAD, GLA_DK_HEAD), F32)

    row = lax.broadcasted_iota(jnp.int32, (tc, tc), 0)
    col = lax.broadcasted_iota(jnp.int32, (tc, tc), 1)
    same_block = (row // GLA_BLOCK) == (col // GLA_BLOCK)
    causal = same_block & (col <= row)
    m_cum = causal.astype(BF16)
    m_tot = same_block.astype(BF16)
    row_block = lax.broadcasted_iota(jnp.int32, (tc, GLA_DK_HEAD), 0) // GLA_BLOCK

    glr = _pad_rows(glr_ref[...], tc).astype(BF16)
    for hh in range(GLA_HEADS):
        ks = slice(hh * GLA_DK_HEAD, (hh + 1) * GLA_DK_HEAD)
        vs = slice(hh * GLA_DV_HEAD, (hh + 1) * GLA_DV_HEAD)
        q = _pad_rows(qk_ref[:, ks], tc) * (GLA_DK_HEAD ** -0.5)
        k = _pad_rows(qk_ref[:, GLA_DK + hh * GLA_DK_HEAD:GLA_DK + (hh + 1) * GLA_DK_HEAD], tc)
        v = _pad_rows(v_ref[:, vs], tc)
        vb = v.astype(BF16)
        v_t = v.T.astype(BF16)

        pre = _dot(glr, wg_ref[:, ks]) + bg_ref[:, ks]
        log_a = _log_sigmoid(pre) / GLA_TAU
        if t_real < tc:
            log_a = jnp.where(lax.broadcasted_iota(jnp.int32, log_a.shape, 0) < t_real, log_a, 0.0)
        la_hi, la_lo = _split_hi_lo(log_a)
        b = _dot(m_cum, la_hi) + _dot(m_cum, la_lo)
        b_tot = _dot(m_tot, la_hi) + _dot(m_tot, la_lo)
        qe = (q * jnp.exp(b)).astype(BF16)
        ke = (k * jnp.exp(-b)).astype(BF16)
        kd = (k * jnp.exp(b_tot - b)).astype(BF16)
        decay = jnp.exp(b_tot)

        att = jnp.where(causal, _dot_nt(qe, ke), 0.0).astype(BF16)
        o_intra = _dot(att, vb)

        st = st_scr[hh]
        outs = []
        for r in range(n_blocks):
            rows = slice(r * GLA_BLOCK, (r + 1) * GLA_BLOCK)
            outs.append(o_intra[rows] + _dot_nt(qe[rows], st.astype(BF16)))
            kd_r = jnp.where(row_block == r, kd, jnp.zeros_like(kd))
            st = decay[r * GLA_BLOCK:r * GLA_BLOCK + 1, :] * st + _dot(v_t, kd_r)
        st_scr[hh] = st

        o = outs[0] if n_blocks == 1 else jnp.concatenate(outs, axis=0)
        o = o * lax.rsqrt(jnp.mean(o * o, axis=-1, keepdims=True) + RMS_EPS) * nw_ref[...]
        g = gog_ref[:, vs]
        og_ref[:, vs] = (o * (g * jax.nn.sigmoid(g))).astype(BF16)

    @pl.when(t == pl.num_programs(1) - 1)
    def _():
        for hh in range(GLA_HEADS):
            sfin_ref[hh] = st_scr[hh].T


def _gla(proj, glr, w_gate, b_gate, norm_w, s0, batch, seq_len):
    n = batch * seq_len
    tc = min(_GLA_CHUNK, seq_len)
    assert seq_len % tc == 0 and tc % GLA_BLOCK == 0
    steps = seq_len // tc
    rows = lambda b, t: (b * steps + t, 0)
    in_specs = [
        pl.BlockSpec((tc, D_MODEL), lambda b, t: (b * steps + t, 0)),
        pl.BlockSpec((tc, D_MODEL), lambda b, t: (b * steps + t, 1)),
        pl.BlockSpec((tc, D_MODEL), lambda b, t: (b * steps + t, 2)),
        pl.BlockSpec((tc, LANES), rows),
        pl.BlockSpec((LANES, GLA_DK), lambda b, t: (0, 0)),
        pl.BlockSpec((1, GLA_DK), lambda b, t: (0, 0)),
        pl.BlockSpec((1, GLA_DV_HEAD), lambda b, t: (0, 0)),
    ]
    operands = [proj, proj, proj, glr, w_gate, b_gate, norm_w]
    state_spec = pl.BlockSpec((None, GLA_HEADS, GLA_DK_HEAD, GLA_DV_HEAD), lambda b, t: (b, 0, 0, 0))
    if s0 is not None:
        in_specs.append(state_spec)
        operands.append(s0)
    return pl.pallas_call(
        functools.partial(_gla_kernel, t_real=tc, has_s0=s0 is not None),
        grid=(batch, steps),
        in_specs=in_specs,
        out_specs=[pl.BlockSpec((tc, D_MODEL), rows), state_spec],
        out_shape=[
            jax.ShapeDtypeStruct((n, D_MODEL), BF16),
            jax.ShapeDtypeStruct((batch, GLA_HEADS, GLA_DK_HEAD, GLA_DV_HEAD), F32),
        ],
        scratch_shapes=[pltpu.VMEM((GLA_HEADS, GLA_DV_HEAD, GLA_DK_HEAD), F32)],
        compiler_params=_params(("arbitrary", "arbitrary")),
        name="gla_recurrence",
    )(*operands)


_SB_TILE = 128
_SB_LANES = 256
_SB_DEAD = -88.0


def _sb_kernel(q_ref, kd_ref, vd_ref, kp_ref, vp_ref, o_ref, *, t_real, past_blocks):
    tq = _SB_TILE
    heads = _SB_LANES // SB_HD
    n_past = pl.program_id(2) if past_blocks is None else past_blocks

    lane = lax.broadcasted_iota(jnp.int32, (tq, _SB_LANES), 1)
    q = _pad_rows(q_ref[...], tq) * (SB_HD ** -0.5)
    in_head = [(lane >= hd * SB_HD) & (lane < (hd + 1) * SB_HD) for hd in range(heads)]
    q_heads = [jnp.where(m, q, 0.0).astype(BF16) for m in in_head]

    r2 = lax.broadcasted_iota(jnp.int32, (2 * tq, 2 * tq), 0)
    c2 = lax.broadcasted_iota(jnp.int32, (2 * tq, 2 * tq), 1)
    suffix = jnp.concatenate([(r2 > c2).astype(BF16), jnp.ones((2 * tq, LANES), BF16)], axis=1)
    row = lax.broadcasted_iota(jnp.int32, (tq, 2 * tq), 0)
    col = lax.broadcasted_iota(jnp.int32, (tq, 2 * tq), 1)

    def one_head(qh, kb, vb, carry, mask):
        z = _dot_nt(qh, kb)
        soft = jnp.log1p(jnp.exp(-jnp.abs(z)))
        log_rest = jnp.where(mask, -jnp.maximum(z, 0.0) - soft, 0.0)
        log_beta = jnp.minimum(z, 0.0) - soft
        hi, lo = _split_hi_lo(log_rest)
        sums = _dot(hi, suffix) + _dot(lo, suffix)
        after = sums[:, :2 * tq]
        total = sums[:, 2 * tq:]
        if carry is not None:
            after = after + jnp.concatenate([carry, carry], axis=1)
            total = total + carry
        a = jnp.where(mask, jnp.exp(log_beta + after), 0.0)
        return _dot(a.astype(BF16), vb), total

    def step(first_rows, second_k, second_v, carries, mask):
        kb = jnp.concatenate([kp_ref[pl.ds(first_rows, tq), :], second_k], axis=0).astype(BF16)
        vb = jnp.concatenate([vp_ref[pl.ds(first_rows, tq), :], second_v], axis=0).astype(BF16)
        acc = None
        new_carries = []
        for hd in range(heads):
            pv, c = one_head(q_heads[hd], kb, vb, None if carries is None else carries[hd], mask)
            new_carries.append(c)
            acc = jnp.where(in_head[hd], pv, 0.0 if acc is None else acc)
        return acc, new_carries

    def live(carries):
        top = carries[0]
        for c in carries[1:]:
            top = jnp.maximum(top, c)
        return (jnp.max(top) > _SB_DEAD).astype(jnp.int32)

    prev_cols = jnp.where(n_past > 0, tq, 0)
    visible = (col < prev_cols) | ((col >= tq) & (col - tq < row))
    prev_start = pl.multiple_of(jnp.maximum(n_past - 1, 0) * tq, tq)
    acc, carries = step(prev_start, _pad_rows(kd_ref[...], tq), _pad_rows(vd_ref[...], tq), None, visible)

    def more(state):
        return (state[0] >= 0) & (state[1] > 0)

    def older(state):
        j, _, acc = state[0], state[1], state[2]
        carries = list(state[3:])
        first = pl.multiple_of(jnp.maximum(j - 1, 0) * tq, tq)
        second = pl.multiple_of(j * tq, tq)
        mask = (col >= jnp.where(j > 0, 0, tq))
        pv, carries = step(first, kp_ref[pl.ds(second, tq), :], vp_ref[pl.ds(second, tq), :], carries, mask)
        return (j - 2, live(carries), acc + pv, *carries)

    state = lax.while_loop(more, older, (n_past - 2, live(carries), acc, *carries))
    o_ref[...] = state[2][:t_real].astype(BF16)


def _stick_breaking(proj, sbk, sbv, k_past, v_past, batch, seq_len, past_len):
    n = batch * seq_len
    tq = min(_SB_TILE, seq_len)
    assert seq_len % tq == 0
    q_tiles = seq_len // tq
    groups = D_MODEL // _SB_LANES
    sq_col0 = 3 * groups
    own = k_past is None
    if own:
        k_past, v_past, past_len = sbk, sbv, seq_len
    assert past_len % _SB_TILE == 0
    tile = lambda b, p, i: (b * q_tiles + i, p)
    whole = lambda b, p, i: (b, p)
    return pl.pallas_call(
        functools.partial(_sb_kernel, t_real=tq, past_blocks=None if own else past_len // _SB_TILE),
        grid=(batch, groups, q_tiles),
        in_specs=[
            pl.BlockSpec((tq, _SB_LANES), lambda b, p, i: (b * q_tiles + i, sq_col0 + p)),
            pl.BlockSpec((tq, _SB_LANES), tile),
            pl.BlockSpec((tq, _SB_LANES), tile),
            pl.BlockSpec((past_len, _SB_LANES), whole),
            pl.BlockSpec((past_len, _SB_LANES), whole),
        ],
        out_specs=pl.BlockSpec((tq, _SB_LANES), tile),
        out_shape=jax.ShapeDtypeStruct((n, D_MODEL), BF16),
        compiler_params=_params(("arbitrary", "arbitrary", "arbitrary")),
        name="stick_breaking",
    )(proj, sbk, sbv, k_past, v_past)


def _merge_kernel(og_ref, osb_ref, mga_ref, mgb_ref, x_ref, g1_ref, sh2_ref, sc2_ref,
                  wa_ref, wb_ref, wo_ref, l1w_ref, l1b_ref, x1_ref, h2_ref):
    br_a = _dot(og_ref[...], wa_ref[...])
    br_b = _dot(osb_ref[...], wb_ref[...])
    mixed = jax.nn.sigmoid(mga_ref[...]) * br_a + jax.nn.sigmoid(mgb_ref[...]) * br_b
    mix = _dot(mixed.astype(BF16), wo_ref[...])
    x1 = _layer_norm(DN_ALPHA * x_ref[...] + g1_ref[...] * mix) * l1w_ref[...] + l1b_ref[...]
    x1_ref[...] = x1
    h2_ref[...] = (_layer_norm(x1) * (1.0 + sc2_ref[...]) + sh2_ref[...]).astype(BF16)


def _merge(og, osb, proj, x2d, g1, sh2, sc2, per_row, seq_len, wa, wb, wo, l1w, l1b):
    n = x2d.shape[0]
    tm = min(512, seq_len) if not per_row else min(512, n)
    tiles_per_seq = max(seq_len // tm, 1)
    mods = [_mod_operand(v, per_row, tm, tiles_per_seq, 1) for v in (g1, sh2, sc2)]
    rows = lambda i: (i, 0)
    full = lambda i: (0, 0)
    wspec = pl.BlockSpec((D_MODEL, D_MODEL), full)
    vspec = pl.BlockSpec((1, D_MODEL), full)
    return pl.pallas_call(
        _merge_kernel,
        grid=(n // tm,),
        in_specs=[
            pl.BlockSpec((tm, D_MODEL), rows),
            pl.BlockSpec((tm, D_MODEL), rows),
            pl.BlockSpec((tm, D_MODEL), lambda i: (i, 4)),
            pl.BlockSpec((tm, D_MODEL), lambda i: (i, 5)),
            pl.BlockSpec((tm, D_MODEL), rows),
            mods[0][1], mods[1][1], mods[2][1],
            wspec, wspec, wspec, vspec, vspec,
        ],
        out_specs=[pl.BlockSpec((tm, D_MODEL), rows), pl.BlockSpec((tm, D_MODEL), rows)],
        out_shape=[jax.ShapeDtypeStruct((n, D_MODEL), F32), jax.ShapeDtypeStruct((n, D_MODEL), BF16)],
        compiler_params=_params(("arbitrary",)),
        name="merge_ln1",
    )(og, osb, proj, proj, x2d, mods[0][0], mods[1][0], mods[2][0], wa, wb, wo, l1w, l1b)


def _extract_top(s, count):
    m = s.shape[0]
    idx = lax.broadcasted_iota(jnp.int32, s.shape, 0).astype(F32)
    rank = lax.broadcasted_iota(jnp.int32, (count, s.shape[1]), 0)

    def step(r, carry):
        cur, taken, vals = carry
        top = jnp.max(cur, axis=0, keepdims=True)
        first = jnp.min(jnp.where(cur == top, idx, float(m)), axis=0, keepdims=True)
        hit = idx == first
        vals = jnp.where(rank == r, top, vals)
        return jnp.where(hit, NEG_INF, cur), jnp.where(hit, 1.0, taken), vals

    init = (s, jnp.zeros(s.shape, F32), jnp.zeros((count, s.shape[1]), F32))
    _, taken, vals = lax.fori_loop(0, count, step, init)
    return vals, taken > 0.5


def _order(xs, i, j):
    xs[i], xs[j] = jnp.maximum(xs[i], xs[j]), jnp.minimum(xs[i], xs[j])


def _bitonic_merge(xs):
    n = len(xs)
    j = n // 2
    while j >= 1:
        for i in range(n):
            if i & j == 0:
                _order(xs, i, i | j)
        j //= 2


def _bitonic_sort(xs):
    n = len(xs)
    k = 2
    while k <= n:
        j = k // 2
        while j >= 1:
            for i in range(n):
                l = i ^ j
                if l > i:
                    if i & k == 0:
                        _order(xs, i, l)
                    else:
                        _order(xs, l, i)
            j //= 2
        k *= 2


def _sublane_all(x, op):
    for shift in (4, 2, 1):
        x = op(x, pltpu.roll(x, shift, 0))
    return x


def _top16_of_tile(blocks):
    xs = list(blocks)
    _bitonic_sort(xs)
    k = len(xs)
    for shift in (4, 2, 1):
        other = [pltpu.roll(x, shift, 0) for x in xs]
        xs = [jnp.maximum(xs[r], other[k - 1 - r]) for r in range(k)]
        _bitonic_merge(xs)
    return xs


_PAIRS = [(a, b) for a in range(PEER_TOPK) for b in range(PEER_TOPK // (a + 1))]


def _route_kernel(h2_ref, wq_ref, keys_ref, tkey_ref, c1_ref, s2m_ref, e2_ref, s_scr, top_scr, in_scr):
    tn = h2_ref.shape[0]
    tiles = tn // LANES
    k = PEER_TOPK
    inf = float("inf")
    q = _dot(h2_ref[...], wq_ref[...]).astype(BF16)
    for p in range(2):
        s_scr[p] = _dot_nt(keys_ref[p], q[:, p * PEER_HALF:(p + 1) * PEER_HALF])

    for p in range(2):
        tied = jnp.zeros((8, LANES), F32)
        for g in range(tiles):
            tl = slice(g * LANES, (g + 1) * LANES)
            blocks = [s_scr[p, 8 * v:8 * v + 8, tl] for v in range(N_KEYS // 8)]
            top = _top16_of_tile(blocks)
            for r in range(k):
                top_scr[p, r, :, tl] = top[r]
            inside = [jnp.where(b >= top[k - 1], 1.0, 0.0) for b in blocks]
            count = inside[0]
            for m in inside[1:]:
                count = count + m
            tied = jnp.maximum(tied, jnp.where(_sublane_all(count, jnp.add) != float(k), 1.0, 0.0))
            for v, m in enumerate(inside):
                in_scr[p, 8 * v:8 * v + 8, tl] = m

        @pl.when(jnp.max(tied) > 0.0)
        def _():
            _, taken = _extract_top(s_scr[p], k)
            in_scr[p] = jnp.where(taken, 1.0, 0.0)

    sub = lax.broadcasted_iota(jnp.int32, (8, LANES), 0)

    def pack(p, r):
        out = top_scr[p, r, :, 0:LANES]
        for g in range(1, tiles):
            out = jnp.where(sub == g, top_scr[p, r, :, g * LANES:(g + 1) * LANES], out)
        return out

    v1 = [pack(0, r) for r in range(k)]
    v2 = [pack(1, r) for r in range(k)]
    cands = [v1[a] + v2[b] for a, b in _PAIRS]
    best = cands[0]
    work = list(cands)
    z = jnp.zeros((8, LANES), F32)
    for _ in range(k):
        top = work[0]
        for c in work[1:]:
            top = jnp.maximum(top, c)
        first = jnp.where(work[0] == top, 0.0, float(len(work)))
        for i, c in enumerate(work[1:], 1):
            first = jnp.minimum(first, jnp.where(c == top, float(i), float(len(work))))
        work = [jnp.where(first == float(i), NEG_INF, c) for i, c in enumerate(work)]
        z = z + jnp.exp(top - best)
    thr = top
    inv_z = 1.0 / z
    t_rank = []
    for a in range(k):
        t = jnp.full((8, LANES), inf, F32)
        for b in range(k // (a + 1)):
            t = jnp.minimum(t, jnp.where(cands[_PAIRS.index((a, b))] >= thr, v2[b], inf))
        t_rank.append(t)

    for g in range(tiles):
        tl = slice(g * LANES, (g + 1) * LANES)
        spread = lambda x: jnp.broadcast_to(x[g:g + 1, :], (8, LANES))
        inv_z_g = spread(inv_z)
        t_g = [spread(t) for t in t_rank]
        v1_g = [top_scr[0, a, :, tl] for a in range(k)]
        m2 = top_scr[1, 0, :, tl]
        for v in range(N_KEYS // 8):
            rows = slice(8 * v, 8 * v + 8)
            s1 = s_scr[0, rows, tl]
            in1 = in_scr[0, rows, tl] > 0.5
            t = jnp.full((8, LANES), inf, F32)
            for a in range(k):
                t = jnp.where(s1 == v1_g[a], t_g[a], t)
            tkey_ref[rows, tl] = jnp.where(in1, t, inf)
            c1_ref[rows, tl] = jnp.where(in1, jnp.exp(s1 - v1_g[0]) * inv_z_g, 0.0)
            s2 = s_scr[1, rows, tl]
            in2 = in_scr[1, rows, tl] > 0.5
            s2m_ref[rows, tl] = jnp.where(in2, s2, NEG_INF)
            e2_ref[rows, tl] = jnp.where(in2, jnp.exp(s2 - m2), 0.0)


_ROUTE_TOKENS = 8 * LANES


def _peer_route(h2, wq, keys):
    n = h2.shape[0]
    tn = min(_ROUTE_TOKENS, n)
    assert n % tn == 0 and tn % LANES == 0
    per_head = pl.BlockSpec((None, N_KEYS, tn), lambda i, h: (h, 0, i))
    shape = jax.ShapeDtypeStruct((PEER_HEADS, N_KEYS, n), F32)
    return pl.pallas_call(
        _route_kernel,
        grid=(n // tn, PEER_HEADS),
        in_specs=[
            pl.BlockSpec((tn, D_MODEL), lambda i, h: (i, 0)),
            pl.BlockSpec((D_MODEL, 2 * PEER_HALF), lambda i, h: (0, h)),
            pl.BlockSpec((None, 2, N_KEYS, PEER_HALF), lambda i, h: (h, 0, 0, 0)),
        ],
        out_specs=[per_head, per_head, per_head, per_head],
        out_shape=[shape, shape, shape, shape],
        scratch_shapes=[
            pltpu.VMEM((2, N_KEYS, tn), F32),
            pltpu.VMEM((2, PEER_TOPK, 8, tn), F32),
            pltpu.VMEM((2, N_KEYS, tn), F32),
        ],
        compiler_params=_params(("arbitrary", "arbitrary")),
        name="peer_route",
    )(h2, wq, keys)


_EXPERT_CHUNK = 1024
_ROWS_PER_CHUNK = _EXPERT_CHUNK // N_KEYS
_EXPERT_SUB = 256
_SUBS_PER_CHUNK = _EXPERT_CHUNK // _EXPERT_SUB


def _gelu_tanh(x):
    c = 2.0 * 0.7978845608028654
    return x / (1.0 + jnp.exp(x * (-c - (c * 0.044715) * (x * x))))


def _expert_kernel(h2_ref, u_ref, vt_ref, tkey_ref, c1_ref, s2m_ref, e2_ref,
                   x1_ref, g2_ref, l2w_ref, l2b_ref, y_ref, acc_scr, w_scr):
    c = pl.program_id(1)
    n_chunks = pl.num_programs(1) - 1
    tokens = w_scr.shape[1]
    keys_per_sub = _EXPERT_SUB // N_KEYS

    @pl.when(c == 0)
    def _():
        acc_scr[...] = jnp.zeros_like(acc_scr)
        w_scr[...] = jnp.zeros_like(w_scr)

    def down(k):
        e0 = pl.multiple_of(k * _EXPERT_SUB, _EXPERT_SUB)
        acc_scr[...] += _dot(vt_ref[k], w_scr[pl.ds(e0, _EXPERT_SUB), :])

    @pl.when(c < n_chunks)
    def _():
        def sub(k, _):
            down(k)
            e0 = pl.multiple_of(k * _EXPERT_SUB, _EXPERT_SUB)
            act = _dot_nt(u_ref[pl.ds(e0, _EXPERT_SUB), :], h2_ref[...])
            for half in range(keys_per_sub):
                ii = k * keys_per_sub + half
                gate = jnp.zeros((N_KEYS, tokens), F32)
                for h in range(PEER_HEADS):
                    chosen = jnp.where(s2m_ref[h] >= tkey_ref[h, pl.ds(ii, 1), :], e2_ref[h], 0.0)
                    gate = gate + chosen * c1_ref[h, pl.ds(ii, 1), :]
                rows = slice(half * N_KEYS, (half + 1) * N_KEYS)
                w_scr[pl.ds(e0 + half * N_KEYS, N_KEYS), :] = (gate * _gelu_tanh(act[rows])).astype(BF16)
            return 0

        lax.fori_loop(0, _SUBS_PER_CHUNK, sub, 0)

    @pl.when(c == n_chunks)
    def _():
        for k in range(_SUBS_PER_CHUNK):
            down(k)
        f = acc_scr[...].T
        y = _layer_norm(DN_ALPHA * x1_ref[...] + g2_ref[...] * f)
        y_ref[...] = y * l2w_ref[...] + l2b_ref[...]


def _peer_experts(h2, u, v_t, route, x1, g2, per_row, seq_len, l2w, l2b):
    n = h2.shape[0]
    tg = min(512, seq_len) if not per_row else min(512, n)
    tiles_per_seq = max(seq_len // tg, 1)
    g2_arr, g2_spec = _mod_operand(g2, per_row, tg, tiles_per_seq, 2)
    tkey, c1, s2m, e2 = route
    n_chunks = N_EXPERTS // _EXPERT_CHUNK
    rows = lambda i, c: (i, 0)
    full = lambda i, c: (0, 0)
    this_chunk = lambda c: jnp.minimum(c, n_chunks - 1)
    chunk_rows = pl.BlockSpec((PEER_HEADS, _ROWS_PER_CHUNK, tg), lambda i, c: (0, this_chunk(c), i))
    all_rows = pl.BlockSpec((PEER_HEADS, N_KEYS, tg), lambda i, c: (0, 0, i))
    return pl.pallas_call(
        _expert_kernel,
        grid=(n // tg, n_chunks + 1),
        in_specs=[
            pl.BlockSpec((tg, D_MODEL), rows),
            pl.BlockSpec((_EXPERT_CHUNK, D_MODEL), lambda i, c: (this_chunk(c), 0)),
            pl.BlockSpec((_SUBS_PER_CHUNK, D_MODEL, _EXPERT_SUB), lambda i, c: (jnp.maximum(c - 1, 0), 0, 0)),
            chunk_rows, chunk_rows, all_rows, all_rows,
            pl.BlockSpec((tg, D_MODEL), rows),
            g2_spec,
            pl.BlockSpec((1, D_MODEL), full),
            pl.BlockSpec((1, D_MODEL), full),
        ],
        out_specs=pl.BlockSpec((tg, D_MODEL), rows),
        out_shape=jax.ShapeDtypeStruct((n, D_MODEL), F32),
        scratch_shapes=[
            pltpu.VMEM((D_MODEL, tg), F32),
            pltpu.VMEM((_EXPERT_CHUNK, tg), BF16),
        ],
        compiler_params=_params(("arbitrary", "arbitrary")),
        name="peer_experts_ln2",
    )(h2, u, v_t, tkey, c1, s2m, e2, x1, g2_arr, l2w, l2b)


def _trunk_layer(x, mod, s0, k_past, v_past, wts):
    batch, seq_len, _ = x.shape
    n = batch * seq_len
    per_row = seq_len < 128
    if per_row:
        mod = jnp.repeat(mod, seq_len, axis=0)
    sh1, sc1, g1, sh2, sc2, g2 = [mod[:, k * D_MODEL:(k + 1) * D_MODEL] for k in range(N_MOD)]
    x2d = x.reshape(n, D_MODEL)

    proj, sbk, sbv, glr = _input_projection(x2d, sh1, sc1, per_row, seq_len, wts["w_main"], wts["w_glr"])
    og, s_fin = _gla(proj, glr, wts["w_gate"], wts["b_gate"], wts["norm_w"], s0, batch, seq_len)
    if k_past is None:
        osb = _stick_breaking(proj, sbk, sbv, None, None, batch, seq_len, 0)
    else:
        past_len = k_past.shape[1]
        osb = _stick_breaking(proj, sbk, sbv, k_past.reshape(batch * past_len, D_MODEL),
                              v_past.reshape(batch * past_len, D_MODEL), batch, seq_len, past_len)
    x1, h2 = _merge(og, osb, proj, x2d, g1, sh2, sc2, per_row, seq_len,
                    wts["w_a"], wts["w_b"], wts["w_o"], wts["ln1_w"], wts["ln1_b"])
    route = _peer_route(h2, wts["peer_wq"], wts["peer_keys"])
    y = _peer_experts(h2, wts["peer_u"], wts["peer_vt"], route, x1, g2, per_row, seq_len,
                      wts["ln2_w"], wts["ln2_b"])
    shape_kv = (batch, seq_len, SB_HEADS, SB_HD)
    return y.reshape(batch, seq_len, D_MODEL), s_fin, sbk.reshape(shape_kv), sbv.reshape(shape_kv)


def kernel(x_prompt, x_sample, state_gla, cache_sb_k, cache_sb_v, c_prompt, c_sample, w_ada, b_ada, w_in, w_gla_gate, b_gla_gate, gla_norm_w, w_gla_proj, w_sb_proj, w_o, ln1_w, ln1_b, peer_wq, peer_keys, peer_u, peer_v, ln2_w, ln2_b):
    assert w_in.shape[0] == DEPTH
    l = 0
    glr0 = 2 * GLA_DK + GLA_HEADS * GLA_DV_HEAD
    w = w_in[l]
    wts = {
        "w_main": jnp.concatenate([w[:, :glr0], w[:, glr0 + GLA_GATE_RANK:]], axis=1).astype(BF16),
        "w_glr": jnp.pad(w[:, glr0:glr0 + GLA_GATE_RANK], ((0, 0), (0, LANES - GLA_GATE_RANK))).astype(BF16),
        "w_gate": jnp.pad(w_gla_gate[l], ((0, LANES - GLA_GATE_RANK), (0, 0))).astype(BF16),
        "b_gate": b_gla_gate[l][None, :],
        "norm_w": gla_norm_w[l][None, :],
        "w_a": w_gla_proj[l].astype(BF16),
        "w_b": w_sb_proj[l].astype(BF16),
        "w_o": w_o[l].astype(BF16),
        "ln1_w": ln1_w[l][None, :],
        "ln1_b": ln1_b[l][None, :],
        "peer_wq": peer_wq[l].astype(BF16),
        "peer_keys": peer_keys[l].astype(BF16),
        "peer_u": peer_u[l].astype(BF16),
        "peer_vt": peer_v[l].astype(BF16).reshape(N_EXPERTS // _EXPERT_SUB, _EXPERT_SUB, D_MODEL).transpose(0, 2, 1),
        "ln2_w": ln2_w[l][None, :],
        "ln2_b": ln2_b[l][None, :],
    }
    n_prompt = c_prompt.shape[0]
    mod = _modulation(jnp.concatenate([c_prompt, c_sample], axis=0), w_ada[l], b_ada[l][None, :])

    yp, sp, kp, vp = _trunk_layer(x_prompt, mod[:n_prompt], None, None, None, wts)
    ys, ss, ksm, vsm = _trunk_layer(x_sample, mod[n_prompt:], state_gla[l], cache_sb_k[l], cache_sb_v[l], wts)
    return (yp, ys, sp[None], kp[None], vp[None], ss[None], ksm[None], vsm[None])
```

```python
import functools

import jax
import jax.numpy as jnp
from jax import lax
from jax.experimental import pallas as pl
from jax.experimental.pallas import tpu as pltpu

F32 = jnp.float32
BF16 = jnp.bfloat16

D_MODEL = 1024
N_MOD = 6
GLA_HEADS = 4
GLA_DK_HEAD = 128
GLA_DV_HEAD = 256
GLA_DK = GLA_HEADS * GLA_DK_HEAD
GLA_GATE_RANK = 16
GLA_TAU = 16.0
GLA_BLOCK = 16
SB_HEADS = 16
SB_HD = 64
SB_PAIRS = SB_HEADS // 2
PEER_HEADS = 8
PEER_HALF = 128
N_KEYS = 128
N_EXPERTS = N_KEYS * N_KEYS
PEER_TOPK = 16
DEPTH = 1
DN_ALPHA = (2.0 * DEPTH) ** 0.25
LN_EPS = 1e-5
RMS_EPS = 1e-6

LANES = 128
VMEM_LIMIT = 48 * 1024 * 1024
NEG_INF = float("-inf")

_NT = (((1,), (1,)), ((), ()))


def _dot(a, b):
    return jnp.dot(a, b, preferred_element_type=F32)


def _dot_nt(a, b):
    return lax.dot_general(a, b, _NT, preferred_element_type=F32)


def _layer_norm(x):
    mu = jnp.mean(x, axis=-1, keepdims=True)
    xc = x - mu
    var = jnp.mean(xc * xc, axis=-1, keepdims=True)
    return xc * lax.rsqrt(var + LN_EPS)


def _log_sigmoid(x):
    return jnp.minimum(x, 0.0) - jnp.log1p(jnp.exp(-jnp.abs(x)))


def _split_hi_lo(x):
    hi = x.astype(BF16)
    lo = (x - hi.astype(F32)).astype(BF16)
    return hi, lo


def _params(semantics):
    return pltpu.CompilerParams(dimension_semantics=semantics, vmem_limit_bytes=VMEM_LIMIT)


def _mod_operand(vec, per_row, tile, tiles_per_seq, grid_rank):
    pad = (0,) * (grid_rank - 1)
    if per_row:
        return vec, pl.BlockSpec((tile, D_MODEL), lambda i, *_: (i, 0))
    arr = vec[:, None, :]
    return arr, pl.BlockSpec((None, 1, D_MODEL), lambda i, *_: (i // tiles_per_seq, 0, 0))


def _mod_kernel(c_ref, w_ref, b_ref, o_ref):
    c = c_ref[...]
    s = (c * jax.nn.sigmoid(c)).astype(BF16)
    o_ref[...] = _dot(s, w_ref[...].astype(BF16)) + b_ref[...]


def _modulation(c_all, w_ada, b_ada):
    rows = c_all.shape[0]
    return pl.pallas_call(
        _mod_kernel,
        grid=(N_MOD,),
        in_specs=[
            pl.BlockSpec((rows, D_MODEL), lambda j: (0, 0)),
            pl.BlockSpec((D_MODEL, D_MODEL), lambda j: (0, j)),
            pl.BlockSpec((1, D_MODEL), lambda j: (0, j)),
        ],
        out_specs=pl.BlockSpec((rows, D_MODEL), lambda j: (0, j)),
        out_shape=jax.ShapeDtypeStruct((rows, N_MOD * D_MODEL), F32),
        compiler_params=_params(("arbitrary",)),
        name="adaln_mod",
    )(c_all, w_ada, b_ada)


_N_COL_TILES = 8
_COL_SK = 4
_COL_SV = 5


def _inproj_kernel(x_ref, sh_ref, sc_ref, w_ref, wglr_ref, proj_ref, sbk_ref, sbv_ref, glr_ref, h_scr):
    j = pl.program_id(1)

    @pl.when(j == 0)
    def _():
        h = _layer_norm(x_ref[...]) * (1.0 + sc_ref[...]) + sh_ref[...]
        hb = h.astype(BF16)
        h_scr[...] = hb
        glr_ref[...] = _dot(hb, wglr_ref[...])

    acc = _dot(h_scr[...], w_ref[...])

    @pl.when(j == _COL_SK)
    def _():
        sbk_ref[...] = acc

    @pl.when(j == _COL_SV)
    def _():
        sbv_ref[...] = acc

    @pl.when((j < _COL_SK) | (j > _COL_SV))
    def _():
        proj_ref[...] = acc


def _proj_col(j):
    return jnp.where(j < _COL_SK, j, jnp.where(j <= _COL_SV, _COL_SK - 1, j - 2))


def _input_projection(x2d, shift, scale, per_row, seq_len, w_main, w_glr):
    n = x2d.shape[0]
    tm = min(1024, seq_len) if not per_row else min(512, n)
    tiles_per_seq = max(seq_len // tm, 1)
    sh_arr, sh_spec = _mod_operand(shift, per_row, tm, tiles_per_seq, 2)
    sc_arr, sc_spec = _mod_operand(scale, per_row, tm, tiles_per_seq, 2)
    row_out = lambda i, j: (i, 0)
    return pl.pallas_call(
        _inproj_kernel,
        grid=(n // tm, _N_COL_TILES),
        in_specs=[
            pl.BlockSpec((tm, D_MODEL), row_out),
            sh_spec,
            sc_spec,
            pl.BlockSpec((D_MODEL, D_MODEL), lambda i, j: (0, j)),
            pl.BlockSpec((D_MODEL, LANES), lambda i, j: (0, 0)),
        ],
        out_specs=[
            pl.BlockSpec((tm, D_MODEL), lambda i, j: (i, _proj_col(j))),
            pl.BlockSpec((tm, D_MODEL), row_out),
            pl.BlockSpec((tm, D_MODEL), row_out),
            pl.BlockSpec((tm, LANES), row_out),
        ],
        out_shape=[
            jax.ShapeDtypeStruct((n, 6 * D_MODEL), F32),
            jax.ShapeDtypeStruct((n, D_MODEL), F32),
            jax.ShapeDtypeStruct((n, D_MODEL), F32),
            jax.ShapeDtypeStruct((n, LANES), F32),
        ],
        scratch_shapes=[pltpu.VMEM((tm, D_MODEL), BF16)],
        compiler_params=_params(("arbitrary", "arbitrary")),
        name="ln_inproj",
    )(x2d, sh_arr, sc_arr, w_main, w_glr)


_GLA_CHUNK = 128


def _pad_rows(x, rows):
    if x.shape[0] == rows:
        return x
    return jnp.concatenate([x, jnp.zeros((rows - x.shape[0], x.shape[1]), x.dtype)], axis=0)


def _gla_kernel(*refs, t_real, has_s0):
    if has_s0:
        (qk_ref, v_ref, gog_ref, glr_ref, wg_ref, bg_ref, nw_ref, s0_ref, og_ref, sfin_ref, st_scr) = refs
    else:
        (qk_ref, v_ref, gog_ref, glr_ref, wg_ref, bg_ref, nw_ref, og_ref, sfin_ref, st_scr) = refs
        s0_ref = None
    t = pl.program_id(1)
    tc = _GLA_CHUNK
    n_blocks = t_real // GLA_BLOCK

    @pl.when(t == 0)
    def _():
        for hh in range(GLA_HEADS):
            if has_s0:
                st_scr[hh] = s0_ref[hh].T
            else:
                st_scr[hh] = jnp.zeros((GLA_DV_HEAD, GLA_DK_HEAD), F32)

    row = lax.broadcasted_iota(jnp.int32, (tc, tc), 0)
    col = lax.broadcasted_iota(jnp.int32, (tc, tc), 1)
    same_block = (row // GLA_BLOCK) == (col // GLA_BLOCK)
    causal = same_block & (col <= row)
    m_cum = causal.astype(BF16)
    m_tot = same_block.astype(BF16)
    row_block = lax.broadcasted_iota(jnp.int32, (tc, GLA_DK_HEAD), 0) // GLA_BLOCK

    glr = _pad_rows(glr_ref[...], tc).astype(BF16)
    for hh in range(GLA_HEADS):
        ks = slice(hh * GLA_DK_HEAD, (hh + 1) * GLA_DK_HEAD)
        vs = slice(hh * GLA_DV_HEAD, (hh + 1) * GLA_DV_HEAD)
        q = _pad_rows(qk_ref[:, ks], tc) * (GLA_DK_HEAD ** -0.5)
        k = _pad_rows(qk_ref[:, GLA_DK + hh * GLA_DK_HEAD:GLA_DK + (hh + 1) * GLA_DK_HEAD], tc)
        v = _pad_rows(v_ref[:, vs], tc)
        vb = v.astype(BF16)
        v_t = v.T.astype(BF16)

        pre = _dot(glr, wg_ref[:, ks]) + bg_ref[:, ks]
        log_a = _log_sigmoid(pre) / GLA_TAU
        if t_real < tc:
            log_a = jnp.where(lax.broadcasted_iota(jnp.int32, log_a.shape, 0) < t_real, log_a, 0.0)
        la_hi, la_lo = _split_hi_lo(log_a)
        b = _dot(m_cum, la_hi) + _dot(m_cum, la_lo)
        b_tot = _dot(m_tot, la_hi) + _dot(m_tot, la_lo)
        qe = (q * jnp.exp(b)).astype(BF16)
        ke = (k * jnp.exp(-b)).astype(BF16)
        kd = (k * jnp.exp(b_tot - b)).astype(BF16)
        decay = jnp.exp(b_tot)

        att = jnp.where(causal, _dot_nt(qe, ke), 0.0).astype(BF16)
        o_intra = _dot(att, vb)

        st = st_scr[hh]
        outs = []
        for r in range(n_blocks):
            rows = slice(r * GLA_BLOCK, (r + 1) * GLA_BLOCK)
            outs.append(o_intra[rows] + _dot_nt(qe[rows], st.astype(BF16)))
            kd_r = jnp.where(row_block == r, kd, jnp.zeros_like(kd))
            st = decay[r * GLA_BLOCK:r * GLA_BLOCK + 1, :] * st + _dot(v_t, kd_r)
        st_scr[hh] = st

        o = outs[0] if n_blocks == 1 else jnp.concatenate(outs, axis=0)
        o = o * lax.rsqrt(jnp.mean(o * o, axis=-1, keepdims=True) + RMS_EPS) * nw_ref[...]
        g = gog_ref[:, vs]
        og_ref[:, vs] = (o * (g * jax.nn.sigmoid(g))).astype(BF16)

    @pl.when(t == pl.num_programs(1) - 1)
    def _():
        for hh in range(GLA_HEADS):
            sfin_ref[hh] = st_scr[hh].T


def _gla(proj, glr, w_gate, b_gate, norm_w, s0, batch, seq_len):
    n = batch * seq_len
    tc = min(_GLA_CHUNK, seq_len)
    assert seq_len % tc == 0 and tc % GLA_BLOCK == 0
    steps = seq_len // tc
    rows = lambda b, t: (b * steps + t, 0)
    in_specs = [
        pl.BlockSpec((tc, D_MODEL), lambda b, t: (b * steps + t, 0)),
        pl.BlockSpec((tc, D_MODEL), lambda b, t: (b * steps + t, 1)),
        pl.BlockSpec((tc, D_MODEL), lambda b, t: (b * steps + t, 2)),
        pl.BlockSpec((tc, LANES), rows),
        pl.BlockSpec((LANES, GLA_DK), lambda b, t: (0, 0)),
        pl.BlockSpec((1, GLA_DK), lambda b, t: (0, 0)),
        pl.BlockSpec((1, GLA_DV_HEAD), lambda b, t: (0, 0)),
    ]
    operands = [proj, proj, proj, glr, w_gate, b_gate, norm_w]
    state_spec = pl.BlockSpec((None, GLA_HEADS, GLA_DK_HEAD, GLA_DV_HEAD), lambda b, t: (b, 0, 0, 0))
    if s0 is not None:
        in_specs.append(state_spec)
        operands.append(s0)
    return pl.pallas_call(
        functools.partial(_gla_kernel, t_real=tc, has_s0=s0 is not None),
        grid=(batch, steps),
        in_specs=in_specs,
        out_specs=[pl.BlockSpec((tc, D_MODEL), rows), state_spec],
        out_shape=[
            jax.ShapeDtypeStruct((n, D_MODEL), BF16),
            jax.ShapeDtypeStruct((batch, GLA_HEADS, GLA_DK_HEAD, GLA_DV_HEAD), F32),
        ],
        scratch_shapes=[pltpu.VMEM((GLA_HEADS, GLA_DV_HEAD, GLA_DK_HEAD), F32)],
        compiler_params=_params(("arbitrary", "arbitrary")),
        name="gla_recurrence",
    )(*operands)


_SB_TILE = 128
_SB_LANES = 256
_SB_DEAD = -88.0


def _sb_kernel(q_ref, kd_ref, vd_ref, kp_ref, vp_ref, o_ref, *, t_real, past_blocks):
    tq = _SB_TILE
    heads = _SB_LANES // SB_HD
    n_past = pl.program_id(2) if past_blocks is None else past_blocks

    lane = lax.broadcasted_iota(jnp.int32, (tq, _SB_LANES), 1)
    q = _pad_rows(q_ref[...], tq) * (SB_HD ** -0.5)
    in_head = [(lane >= hd * SB_HD) & (lane < (hd + 1) * SB_HD) for hd in range(heads)]
    q_heads = [jnp.where(m, q, 0.0).astype(BF16) for m in in_head]

    r2 = lax.broadcasted_iota(jnp.int32, (2 * tq, 2 * tq), 0)
    c2 = lax.broadcasted_iota(jnp.int32, (2 * tq, 2 * tq), 1)
    suffix = jnp.concatenate([(r2 > c2).astype(BF16), jnp.ones((2 * tq, LANES), BF16)], axis=1)
    row = lax.broadcasted_iota(jnp.int32, (tq, 2 * tq), 0)
    col = lax.broadcasted_iota(jnp.int32, (tq, 2 * tq), 1)

    def one_head(qh, kb, vb, carry, mask):
        z = _dot_nt(qh, kb)
        soft = jnp.log1p(jnp.exp(-jnp.abs(z)))
        log_rest = jnp.where(mask, -jnp.maximum(z, 0.0) - soft, 0.0)
        log_beta = jnp.minimum(z, 0.0) - soft
        hi, lo = _split_hi_lo(log_rest)
        sums = _dot(hi, suffix) + _dot(lo, suffix)
        after = sums[:, :2 * tq]
        total = sums[:, 2 * tq:]
        if carry is not None:
            after = after + jnp.concatenate([carry, carry], axis=1)
            total = total + carry
        a = jnp.where(mask, jnp.exp(log_beta + after), 0.0)
        return _dot(a.astype(BF16), vb), total

    def step(first_rows, second_k, second_v, carries, mask):
        kb = jnp.concatenate([kp_ref[pl.ds(first_rows, tq), :], second_k], axis=0).astype(BF16)
        vb = jnp.concatenate([vp_ref[pl.ds(first_rows, tq), :], second_v], axis=0).astype(BF16)
        acc = None
        new_carries = []
        for hd in range(heads):
            pv, c = one_head(q_heads[hd], kb, vb, None if carries is None else carries[hd], mask)
            new_carries.append(c)
            acc = jnp.where(in_head[hd], pv, 0.0 if acc is None else acc)
        return acc, new_carries

    def live(carries):
        top = carries[0]
        for c in carries[1:]:
            top = jnp.maximum(top, c)
        return (jnp.max(top) > _SB_DEAD).astype(jnp.int32)

    prev_cols = jnp.where(n_past > 0, tq, 0)
    visible = (col < prev_cols) | ((col >= tq) & (col - tq < row))
    prev_start = pl.multiple_of(jnp.maximum(n_past - 1, 0) * tq, tq)
    acc, carries = step(prev_start, _pad_rows(kd_ref[...], tq), _pad_rows(vd_ref[...], tq), None, visible)

    def more(state):
        return (state[0] >= 0) & (state[1] > 0)

    def older(state):
        j, _, acc = state[0], state[1], state[2]
        carries = list(state[3:])
        first = pl.multiple_of(jnp.maximum(j - 1, 0) * tq, tq)
        second = pl.multiple_of(j * tq, tq)
        mask = (col >= jnp.where(j > 0, 0, tq))
        pv, carries = step(first, kp_ref[pl.ds(second, tq), :], vp_ref[pl.ds(second, tq), :], carries, mask)
        return (j - 2, live(carries), acc + pv, *carries)

    state = lax.while_loop(more, older, (n_past - 2, live(carries), acc, *carries))
    o_ref[...] = state[2][:t_real].astype(BF16)


def _stick_breaking(proj, sbk, sbv, k_past, v_past, batch, seq_len, past_len):
    n = batch * seq_len
    tq = min(_SB_TILE, seq_len)
    assert seq_len % tq == 0
    q_tiles = seq_len // tq
    groups = D_MODEL // _SB_LANES
    sq_col0 = 3 * groups
    own = k_past is None
    if own:
        k_past, v_past, past_len = sbk, sbv, seq_len
    assert past_len % _SB_TILE == 0
    tile = lambda b, p, i: (b * q_tiles + i, p)
    whole = lambda b, p, i: (b, p)
    return pl.pallas_call(
        functools.partial(_sb_kernel, t_real=tq, past_blocks=None if own else past_len // _SB_TILE),
        grid=(batch, groups, q_tiles),
        in_specs=[
            pl.BlockSpec((tq, _SB_LANES), lambda b, p, i: (b * q_tiles + i, sq_col0 + p)),
            pl.BlockSpec((tq, _SB_LANES), tile),
            pl.BlockSpec((tq, _SB_LANES), tile),
            pl.BlockSpec((past_len, _SB_LANES), whole),
            pl.BlockSpec((past_len, _SB_LANES), whole),
        ],
        out_specs=pl.BlockSpec((tq, _SB_LANES), tile),
        out_shape=jax.ShapeDtypeStruct((n, D_MODEL), BF16),
        compiler_params=_params(("arbitrary", "arbitrary", "arbitrary")),
        name="stick_breaking",
    )(proj, sbk, sbv, k_past, v_past)


def _merge_kernel(og_ref, osb_ref, mga_ref, mgb_ref, x_ref, g1_ref, sh2_ref, sc2_ref,
                  wa_ref, wb_ref, wo_ref, l1w_ref, l1b_ref, x1_ref, h2_ref):
    br_a = _dot(og_ref[...], wa_ref[...])
    br_b = _dot(osb_ref[...], wb_ref[...])
    mixed = jax.nn.sigmoid(mga_ref[...]) * br_a + jax.nn.sigmoid(mgb_ref[...]) * br_b
    mix = _dot(mixed.astype(BF16), wo_ref[...])
    x1 = _layer_norm(DN_ALPHA * x_ref[...] + g1_ref[...] * mix) * l1w_ref[...] + l1b_ref[...]
    x1_ref[...] = x1
    h2_ref[...] = (_layer_norm(x1) * (1.0 + sc2_ref[...]) + sh2_ref[...]).astype(BF16)


def _merge(og, osb, proj, x2d, g1, sh2, sc2, per_row, seq_len, wa, wb, wo, l1w, l1b):
    n = x2d.shape[0]
    tm = min(512, seq_len) if not per_row else min(512, n)
    tiles_per_seq = max(seq_len // tm, 1)
    mods = [_mod_operand(v, per_row, tm, tiles_per_seq, 1) for v in (g1, sh2, sc2)]
    rows = lambda i: (i, 0)
    full = lambda i: (0, 0)
    wspec = pl.BlockSpec((D_MODEL, D_MODEL), full)
    vspec = pl.BlockSpec((1, D_MODEL), full)
    return pl.pallas_call(
        _merge_kernel,
        grid=(n // tm,),
        in_specs=[
            pl.BlockSpec((tm, D_MODEL), rows),
            pl.BlockSpec((tm, D_MODEL), rows),
            pl.BlockSpec((tm, D_MODEL), lambda i: (i, 4)),
            pl.BlockSpec((tm, D_MODEL), lambda i: (i, 5)),
            pl.BlockSpec((tm, D_MODEL), rows),
            mods[0][1], mods[1][1], mods[2][1],
            wspec, wspec, wspec, vspec, vspec,
        ],
        out_specs=[pl.BlockSpec((tm, D_MODEL), rows), pl.BlockSpec((tm, D_MODEL), rows)],
        out_shape=[jax.ShapeDtypeStruct((n, D_MODEL), F32), jax.ShapeDtypeStruct((n, D_MODEL), BF16)],
        compiler_params=_params(("arbitrary",)),
        name="merge_ln1",
    )(og, osb, proj, proj, x2d, mods[0][0], mods[1][0], mods[2][0], wa, wb, wo, l1w, l1b)


def _extract_top(s, count):
    m = s.shape[0]
    idx = lax.broadcasted_iota(jnp.int32, s.shape, 0).astype(F32)
    rank = lax.broadcasted_iota(jnp.int32, (count, s.shape[1]), 0)

    def step(r, carry):
        cur, taken, vals = carry
        top = jnp.max(cur, axis=0, keepdims=True)
        first = jnp.min(jnp.where(cur == top, idx, float(m)), axis=0, keepdims=True)
        hit = idx == first
        vals = jnp.where(rank == r, top, vals)
        return jnp.where(hit, NEG_INF, cur), jnp.where(hit, 1.0, taken), vals

    init = (s, jnp.zeros(s.shape, F32), jnp.zeros((count, s.shape[1]), F32))
    _, taken, vals = lax.fori_loop(0, count, step, init)
    return vals, taken > 0.5


def _order(xs, i, j):
    xs[i], xs[j] = jnp.maximum(xs[i], xs[j]), jnp.minimum(xs[i], xs[j])


def _bitonic_merge(xs):
    n = len(xs)
    j = n // 2
    while j >= 1:
        for i in range(n):
            if i & j == 0:
                _order(xs, i, i | j)
        j //= 2


def _bitonic_sort(xs):
    n = len(xs)
    k = 2
    while k <= n:
        j = k // 2
        while j >= 1:
            for i in range(n):
                l = i ^ j
                if l > i:
                    if i & k == 0:
                        _order(xs, i, l)
                    else:
                        _order(xs, l, i)
            j //= 2
        k *= 2


def _sublane_all(x, op):
    for shift in (4, 2, 1):
        x = op(x, pltpu.roll(x, shift, 0))
    return x


def _top16_of_tile(blocks):
    xs = list(blocks)
    _bitonic_sort(xs)
    k = len(xs)
    for shift in (4, 2, 1):
        other = [pltpu.roll(x, shift, 0) for x in xs]
        xs = [jnp.maximum(xs[r], other[k - 1 - r]) for r in range(k)]
        _bitonic_merge(xs)
    return xs


_PAIRS = [(a, b) for a in range(PEER_TOPK) for b in range(PEER_TOPK // (a + 1))]


def _route_kernel(h2_ref, wq_ref, keys_ref, tkey_ref, c1_ref, s2m_ref, e2_ref, s_scr, top_scr, in_scr):
    tn = h2_ref.shape[0]
    tiles = tn // LANES
    k = PEER_TOPK
    inf = float("inf")
    q = _dot(h2_ref[...], wq_ref[...]).astype(BF16)
    for p in range(2):
        s_scr[p] = _dot_nt(keys_ref[p], q[:, p * PEER_HALF:(p + 1) * PEER_HALF])

    for p in range(2):
        tied = jnp.zeros((8, LANES), F32)
        for g in range(tiles):
            tl = slice(g * LANES, (g + 1) * LANES)
            blocks = [s_scr[p, 8 * v:8 * v + 8, tl] for v in range(N_KEYS // 8)]
            top = _top16_of_tile(blocks)
            for r in range(k):
                top_scr[p, r, :, tl] = top[r]
            inside = [jnp.where(b >= top[k - 1], 1.0, 0.0) for b in blocks]
            count = inside[0]
            for m in inside[1:]:
                count = count + m
            tied = jnp.maximum(tied, jnp.where(_sublane_all(count, jnp.add) != float(k), 1.0, 0.0))
            for v, m in enumerate(inside):
                in_scr[p, 8 * v:8 * v + 8, tl] = m

        @pl.when(jnp.max(tied) > 0.0)
        def _():
            _, taken = _extract_top(s_scr[p], k)
            in_scr[p] = jnp.where(taken, 1.0, 0.0)

    sub = lax.broadcasted_iota(jnp.int32, (8, LANES), 0)

    def pack(p, r):
        out = top_scr[p, r, :, 0:LANES]
        for g in range(1, tiles):
            out = jnp.where(sub == g, top_scr[p, r, :, g * LANES:(g + 1) * LANES], out)
        return out

    v1 = [pack(0, r) for r in range(k)]
    v2 = [pack(1, r) for r in range(k)]
    cands = [v1[a] + v2[b] for a, b in _PAIRS]
    best = cands[0]
    work = list(cands)
    z = jnp.zeros((8, LANES), F32)
    for _ in range(k):
        top = work[0]
        for c in work[1:]:
            top = jnp.maximum(top, c)
        first = jnp.where(work[0] == top, 0.0, float(len(work)))
        for i, c in enumerate(work[1:], 1):
            first = jnp.minimum(first, jnp.where(c == top, float(i), float(len(work))))
        work = [jnp.where(first == float(i), NEG_INF, c) for i, c in enumerate(work)]
        z = z + jnp.exp(top - best)
    thr = top
    inv_z = 1.0 / z
    t_rank = []
    for a in range(k):
        t = jnp.full((8, LANES), inf, F32)
        for b in range(k // (a + 1)):
            t = jnp.minimum(t, jnp.where(cands[_PAIRS.index((a, b))] >= thr, v2[b], inf))
        t_rank.append(t)

    for g in range(tiles):
        tl = slice(g * LANES, (g + 1) * LANES)
        spread = lambda x: jnp.broadcast_to(x[g:g + 1, :], (8, LANES))
        inv_z_g = spread(inv_z)
        t_g = [spread(t) for t in t_rank]
        v1_g = [top_scr[0, a, :, tl] for a in range(k)]
        m2 = top_scr[1, 0, :, tl]
        for v in range(N_KEYS // 8):
            rows = slice(8 * v, 8 * v + 8)
            s1 = s_scr[0, rows, tl]
            in1 = in_scr[0, rows, tl] > 0.5
            t = jnp.full((8, LANES), inf, F32)
            for a in range(k):
                t = jnp.where(s1 == v1_g[a], t_g[a], t)
            tkey_ref[rows, tl] = jnp.where(in1, t, inf)
            c1_ref[rows, tl] = jnp.where(in1, jnp.exp(s1 - v1_g[0]) * inv_z_g, 0.0)
            s2 = s_scr[1, rows, tl]
            in2 = in_scr[1, rows, tl] > 0.5
            s2m_ref[rows, tl] = jnp.where(in2, s2, NEG_INF)
            e2_ref[rows, tl] = jnp.where(in2, jnp.exp(s2 - m2), 0.0)


_ROUTE_TOKENS = 8 * LANES


def _peer_route(h2, wq, keys):
    n = h2.shape[0]
    tn = min(_ROUTE_TOKENS, n)
    assert n % tn == 0 and tn % LANES == 0
    per_head = pl.BlockSpec((None, N_KEYS, tn), lambda i, h: (h, 0, i))
    shape = jax.ShapeDtypeStruct((PEER_HEADS, N_KEYS, n), F32)
    return pl.pallas_call(
        _route_kernel,
        grid=(n // tn, PEER_HEADS),
        in_specs=[
            pl.BlockSpec((tn, D_MODEL), lambda i, h: (i, 0)),
            pl.BlockSpec((D_MODEL, 2 * PEER_HALF), lambda i, h: (0, h)),
            pl.BlockSpec((None, 2, N_KEYS, PEER_HALF), lambda i, h: (h, 0, 0, 0)),
        ],
        out_specs=[per_head, per_head, per_head, per_head],
        out_shape=[shape, shape, shape, shape],
        scratch_shapes=[
            pltpu.VMEM((2, N_KEYS, tn), F32),
            pltpu.VMEM((2, PEER_TOPK, 8, tn), F32),
            pltpu.VMEM((2, N_KEYS, tn), F32),
        ],
        compiler_params=_params(("arbitrary", "arbitrary")),
        name="peer_route",
    )(h2, wq, keys)


_EXPERT_CHUNK = 1024
_ROWS_PER_CHUNK = _EXPERT_CHUNK // N_KEYS
_EXPERT_SUB = 256
_SUBS_PER_CHUNK = _EXPERT_CHUNK // _EXPERT_SUB


def _gelu_tanh(x):
    c = 2.0 * 0.7978845608028654
    return x / (1.0 + jnp.exp(x * (-c - (c * 0.044715) * (x * x))))


def _expert_kernel(h2_ref, u_ref, vt_ref, tkey_ref, c1_ref, s2m_ref, e2_ref,
                   x1_ref, g2_ref, l2w_ref, l2b_ref, y_ref, acc_scr, w_scr):
    c = pl.program_id(1)
    tokens = w_scr.shape[1]
    keys_per_sub = _EXPERT_SUB // N_KEYS

    @pl.when(c == 0)
    def _():
        acc_scr[...] = jnp.zeros_like(acc_scr)

    def sub(k, _):
        e0 = pl.multiple_of(k * _EXPERT_SUB, _EXPERT_SUB)
        act = _dot_nt(u_ref[pl.ds(e0, _EXPERT_SUB), :], h2_ref[...])
        for half in range(keys_per_sub):
            ii = k * keys_per_sub + half
            gate = jnp.zeros((N_KEYS, tokens), F32)
            for h in range(PEER_HEADS):
                chosen = jnp.where(s2m_ref[h] >= tkey_ref[h, pl.ds(ii, 1), :], e2_ref[h], 0.0)
                gate = gate + chosen * c1_ref[h, pl.ds(ii, 1), :]
            rows = slice(half * N_KEYS, (half + 1) * N_KEYS)
            w_scr[pl.ds(e0 + half * N_KEYS, N_KEYS), :] = (gate * _gelu_tanh(act[rows])).astype(BF16)
        return 0

    lax.fori_loop(0, _SUBS_PER_CHUNK, sub, 0)
    acc_scr[...] += _dot(vt_ref[...], w_scr[...])

    @pl.when(c == pl.num_programs(1) - 1)
    def _():
        f = acc_scr[...].T
        y = _layer_norm(DN_ALPHA * x1_ref[...] + g2_ref[...] * f)
        y_ref[...] = y * l2w_ref[...] + l2b_ref[...]


def _peer_experts(h2, u, v_t, route, x1, g2, per_row, seq_len, l2w, l2b):
    n = h2.shape[0]
    tg = min(512, seq_len) if not per_row else min(512, n)
    tiles_per_seq = max(seq_len // tg, 1)
    g2_arr, g2_spec = _mod_operand(g2, per_row, tg, tiles_per_seq, 2)
    tkey, c1, s2m, e2 = route
    rows = lambda i, c: (i, 0)
    full = lambda i, c: (0, 0)
    chunk_rows = pl.BlockSpec((PEER_HEADS, _ROWS_PER_CHUNK, tg), lambda i, c: (0, c, i))
    all_rows = pl.BlockSpec((PEER_HEADS, N_KEYS, tg), lambda i, c: (0, 0, i))
    return pl.pallas_call(
        _expert_kernel,
        grid=(n // tg, N_EXPERTS // _EXPERT_CHUNK),
        in_specs=[
            pl.BlockSpec((tg, D_MODEL), rows),
            pl.BlockSpec((_EXPERT_CHUNK, D_MODEL), lambda i, c: (c, 0)),
            pl.BlockSpec((D_MODEL, _EXPERT_CHUNK), lambda i, c: (0, c)),
            chunk_rows, chunk_rows, all_rows, all_rows,
            pl.BlockSpec((tg, D_MODEL), rows),
            g2_spec,
            pl.BlockSpec((1, D_MODEL), full),
            pl.BlockSpec((1, D_MODEL), full),
        ],
        out_specs=pl.BlockSpec((tg, D_MODEL), rows),
        out_shape=jax.ShapeDtypeStruct((n, D_MODEL), F32),
        scratch_shapes=[
            pltpu.VMEM((D_MODEL, tg), F32),
            pltpu.VMEM((_EXPERT_CHUNK, tg), BF16),
        ],
        compiler_params=_params(("arbitrary", "arbitrary")),
        name="peer_experts_ln2",
    )(h2, u, v_t, tkey, c1, s2m, e2, x1, g2_arr, l2w, l2b)


def _trunk_layer(x, mod, s0, k_past, v_past, wts):
    batch, seq_len, _ = x.shape
    n = batch * seq_len
    per_row = seq_len < 128
    if per_row:
        mod = jnp.repeat(mod, seq_len, axis=0)
    sh1, sc1, g1, sh2, sc2, g2 = [mod[:, k * D_MODEL:(k + 1) * D_MODEL] for k in range(N_MOD)]
    x2d = x.reshape(n, D_MODEL)

    proj, sbk, sbv, glr = _input_projection(x2d, sh1, sc1, per_row, seq_len, wts["w_main"], wts["w_glr"])
    og, s_fin = _gla(proj, glr, wts["w_gate"], wts["b_gate"], wts["norm_w"], s0, batch, seq_len)
    if k_past is None:
        osb = _stick_breaking(proj, sbk, sbv, None, None, batch, seq_len, 0)
    else:
        past_len = k_past.shape[1]
        osb = _stick_breaking(proj, sbk, sbv, k_past.reshape(batch * past_len, D_MODEL),
                              v_past.reshape(batch * past_len, D_MODEL), batch, seq_len, past_len)
    x1, h2 = _merge(og, osb, proj, x2d, g1, sh2, sc2, per_row, seq_len,
                    wts["w_a"], wts["w_b"], wts["w_o"], wts["ln1_w"], wts["ln1_b"])
    route = _peer_route(h2, wts["peer_wq"], wts["peer_keys"])
    y = _peer_experts(h2, wts["peer_u"], wts["peer_vt"], route, x1, g2, per_row, seq_len,
                      wts["ln2_w"], wts["ln2_b"])
    shape_kv = (batch, seq_len, SB_HEADS, SB_HD)
    return y.reshape(batch, seq_len, D_MODEL), s_fin, sbk.reshape(shape_kv), sbv.reshape(shape_kv)


def kernel(x_prompt, x_sample, state_gla, cache_sb_k, cache_sb_v, c_prompt, c_sample, w_ada, b_ada, w_in, w_gla_gate, b_gla_gate, gla_norm_w, w_gla_proj, w_sb_proj, w_o, ln1_w, ln1_b, peer_wq, peer_keys, peer_u, peer_v, ln2_w, ln2_b):
    assert w_in.shape[0] == DEPTH
    l = 0
    glr0 = 2 * GLA_DK + GLA_HEADS * GLA_DV_HEAD
    w = w_in[l]
    wts = {
        "w_main": jnp.concatenate([w[:, :glr0], w[:, glr0 + GLA_GATE_RANK:]], axis=1).astype(BF16),
        "w_glr": jnp.pad(w[:, glr0:glr0 + GLA_GATE_RANK], ((0, 0), (0, LANES - GLA_GATE_RANK))).astype(BF16),
        "w_gate": jnp.pad(w_gla_gate[l], ((0, LANES - GLA_GATE_RANK), (0, 0))).astype(BF16),
        "b_gate": b_gla_gate[l][None, :],
        "norm_w": gla_norm_w[l][None, :],
        "w_a": w_gla_proj[l].astype(BF16),
        "w_b": w_sb_proj[l].astype(BF16),
        "w_o": w_o[l].astype(BF16),
        "ln1_w": ln1_w[l][None, :],
        "ln1_b": ln1_b[l][None, :],
        "peer_wq": peer_wq[l].astype(BF16),
        "peer_keys": peer_keys[l].astype(BF16),
        "peer_u": peer_u[l].astype(BF16),
        "peer_vt": peer_v[l].astype(BF16).T,
        "ln2_w": ln2_w[l][None, :],
        "ln2_b": ln2_b[l][None, :],
    }
    n_prompt = c_prompt.shape[0]
    mod = _modulation(jnp.concatenate([c_prompt, c_sample], axis=0), w_ada[l], b_ada[l][None, :])

    yp, sp, kp, vp = _trunk_layer(x_prompt, mod[:n_prompt], None, None, None, wts)
    ys, ss, ksm, vsm = _trunk_layer(x_sample, mod[n_prompt:], state_gla[l], cache_sb_k[l], cache_sb_v[l], wts)
    return (yp, ys, sp[None], kp[None], vp[None], ss[None], ksm[None], vsm[None])
```
